```python
import math
import jax, jax.numpy as jnp
from jax import lax
import numpy as np

D_MODEL = 1024
BATCH = 32
SEQ = 2048
DEPTH = 2
DEC_BATCH = 16
DEC_SEQ = 2048
PAST_LEN = 128

GLA_HEADS = 4
GLA_DK = 64
GLA_DV = 128
GLA_QK_W = GLA_HEADS * GLA_DK
GLA_V_W = GLA_HEADS * GLA_DV
GATE_RANK = 16
GATE_TAU = 16.0
CHUNK = 64
NAT_HEADS = 8
NAT_HD = 64
NAT_W = NAT_HEADS * NAT_HD
GRID_W = 64
MAX_KR = 8
KC = 16
QB = 16
KB = 32
PROJ_SIZES = (GLA_QK_W, GLA_QK_W, GLA_V_W, GLA_V_W, 2 * GATE_RANK, NAT_W, NAT_W, NAT_W)
PROJ_W = 3104
MIX_W = GLA_V_W + NAT_W
FFN_HIDDEN = ((8 * D_MODEL + 3 * 256 - 1) // (3 * 256)) * 256
ALPHA = (2 * DEPTH) ** 0.25
BETA = (8 * DEPTH) ** -0.25
LN_EPS = 1e-5

kernel_name = "hybrid_gla_natten_deepnorm_encoder"


def _layer_norm(x, g, b):
    xf = x.astype(jnp.float32)
    mu = jnp.mean(xf, axis=-1, keepdims=True)
    var = jnp.mean(jnp.square(xf - mu), axis=-1, keepdims=True)
    y = (xf - mu) * lax.rsqrt(var + LN_EPS) * g.astype(jnp.float32) + b.astype(jnp.float32)
    return y.astype(x.dtype)


def _gla_chunked(q, k, v, g, include_diag):
    B, H, L, dk = q.shape
    dv = v.shape[-1]
    n = L // CHUNK
    f32 = jnp.float32
    q = q.astype(f32).reshape(B, H, n, CHUNK, dk)
    k = k.astype(f32).reshape(B, H, n, CHUNK, dk)
    v = v.astype(f32).reshape(B, H, n, CHUNK, dv)
    g = g.astype(f32).reshape(B, H, n, CHUNK, dk)
    bcum = jnp.cumsum(g, axis=3)
    b_last = bcum[:, :, :, -1:, :]
    q_dec = q * jnp.exp(bcum)
    k_inv = k * jnp.exp(-bcum)
    k_end = k * jnp.exp(b_last - bcum)
    mask = jnp.tril(jnp.ones((CHUNK, CHUNK), dtype=bool), k=0 if include_diag else -1)
    scores = jnp.einsum('bhncd,bhnsd->bhncs', q_dec, k_inv)
    intra = jnp.einsum('bhncs,bhnsv->bhncv', jnp.where(mask, scores, 0.0), v)
    dS = jnp.einsum('bhnsd,bhnsv->bhndv', k_end, v)
    decay = jnp.exp(b_last[:, :, :, 0, :])

    def step(S, inp):
        dec, ds = inp
        return dec[..., None] * S + ds, S

    S0 = jnp.zeros((B, H, dk, dv), f32)
    _, S_before = lax.scan(step, S0, (jnp.moveaxis(decay, 2, 0), jnp.moveaxis(dS, 2, 0)))
    S_before = jnp.moveaxis(S_before, 0, 2)
    inter = jnp.einsum('bhncd,bhndv->bhncv', q_dec, S_before)
    return (intra + inter).reshape(B, H, L, dv)


def _nat_col_tables():
    nqb = GRID_W // QB
    qcols = np.arange(GRID_W).reshape(nqb, QB)
    col_start = np.clip(qcols - KC // 2, 0, GRID_W - KC)
    blk_start = np.clip(np.arange(nqb) * QB - KC // 2, 0, GRID_W - KB)
    key_cols = blk_start[:, None] + np.arange(KB)
    kc3 = key_cols[:, None, :]
    valid = (kc3 >= col_start[:, :, None]) & (kc3 < col_start[:, :, None] + KC)
    co_idx = np.clip(kc3 - qcols[:, :, None] + KC - 1, 0, 2 * KC - 2)
    return (jnp.asarray(key_cols, jnp.int32), jnp.asarray(valid), jnp.asarray(co_idx, jnp.int32))


def _neighborhood_attention(q, k, v, rpb):
    B, L, H, d = q.shape
    rows = L // GRID_W
    kr = min(MAX_KR, rows)
    nqb = GRID_W // QB
    key_cols, valid, co_idx = _nat_col_tables()
    qg = (q * (d ** -0.5)).reshape(B, rows, GRID_W, H, d)
    kg = k.reshape(B, rows, GRID_W, H, d)
    vg = v.reshape(B, rows, GRID_W, H, d)

    def row_fn(r):
        rs = jnp.clip(r - kr // 2, 0, rows - kr)
        k_rows = lax.dynamic_slice_in_dim(kg, rs, kr, axis=1)
        v_rows = lax.dynamic_slice_in_dim(vg, rs, kr, axis=1)
        k_blk = k_rows[:, :, key_cols]
        v_blk = v_rows[:, :, key_cols]
        q_row = lax.dynamic_index_in_dim(qg, r, axis=1, keepdims=False).reshape(B, nqb, QB, H, d)
        s = jnp.einsum('bjqhd,bkjmhd->bhjqkm', q_row, k_blk).astype(jnp.float32)
        ro = rs + jnp.arange(kr) - r + (MAX_KR - 1)
        bias = rpb[:, ro[:, None, None, None], co_idx[None]]
        s = s + jnp.transpose(bias, (0, 2, 3, 1, 4)).astype(jnp.float32)[None]
        s = jnp.where(valid[None, None, :, :, None, :], s, -1e30)
        p = jax.nn.softmax(s.reshape(B, H, nqb, QB, kr * KB), axis=-1).reshape(s.shape)
        o = jnp.einsum('bhjqkm,bkjmhd->bjqhd', p.astype(v.dtype), v_blk)
        return o.reshape(B, GRID_W, H, d)

    out = lax.map(row_fn, jnp.arange(rows))
    return jnp.transpose(out, (1, 0, 2, 3, 4)).reshape(B, L, H * d)


def _token_mixer(x, w_in, gate_w2, gate_b, gla_norm_g, rpb, w_out):
    B, L, _ = x.shape
    f32 = jnp.float32
    proj = x @ w_in
    points = [int(p) for p in np.cumsum(PROJ_SIZES)[:-1]]
    q_a, k_a, v_a, r_a, lr, q_b, k_b, v_b = jnp.split(proj, points, axis=-1)

    def heads(t, h, dh):
        return t.reshape(B, L, h, dh).transpose(0, 2, 1, 3)

    qa = heads(q_a, GLA_HEADS, GLA_DK) * (GLA_DK ** -0.5)
    ka = heads(k_a, GLA_HEADS, GLA_DK)
    va = heads(v_a, GLA_HEADS, GLA_DV)
    lr = lr.astype(f32)
    g_f = jax.nn.log_sigmoid(lr[..., :GATE_RANK] @ gate_w2[0].astype(f32) + gate_b[0].astype(f32)) / GATE_TAU
    g_b = jax.nn.log_sigmoid(lr[..., GATE_RANK:] @ gate_w2[1].astype(f32) + gate_b[1].astype(f32)) / GATE_TAU
    g_f = heads(g_f, GLA_HEADS, GLA_DK)
    g_b = heads(g_b, GLA_HEADS, GLA_DK)
    o_fwd = _gla_chunked(qa, ka, va, g_f, True)
    o_bwd = _gla_chunked(qa[:, :, ::-1], ka[:, :, ::-1], va[:, :, ::-1], g_b[:, :, ::-1], False)[:, :, ::-1]
    o = (o_fwd + o_bwd).transpose(0, 2, 1, 3)
    o = o * lax.rsqrt(jnp.mean(jnp.square(o), axis=-1, keepdims=True) + LN_EPS)
    o = o * gla_norm_g.astype(f32).reshape(GLA_HEADS, GLA_DV)
    o_a = (o.reshape(B, L, GLA_V_W) * jax.nn.silu(r_a.astype(f32))).astype(x.dtype)

    o_n = _neighborhood_attention(q_b.reshape(B, L, NAT_HEADS, NAT_HD),
                                  k_b.reshape(B, L, NAT_HEADS, NAT_HD),
                                  v_b.reshape(B, L, NAT_HEADS, NAT_HD), rpb)
    return jnp.concatenate([o_a, o_n.astype(x.dtype)], axis=-1) @ w_out


def _swiglu(x, w_ffn_in, w_ffn_out):
    h = x @ w_ffn_in
    gate, up = jnp.split(h, 2, axis=-1)
    return (jax.nn.silu(gate) * up) @ w_ffn_out


def _trunk(x, w_in, gla_gate_w2, gla_gate_b, gla_norm_g, nat_rpb, w_out,
           ln1_g, ln1_b, w_ffn_in, w_ffn_out, ln2_g, ln2_b):
    for l in range(DEPTH):
        mix = _token_mixer(x, w_in[l], gla_gate_w2[l], gla_gate_b[l], gla_norm_g[l], nat_rpb[l], w_out[l])
        x = _layer_norm(ALPHA * x + mix, ln1_g[l], ln1_b[l])
        x = _layer_norm(ALPHA * x + _swiglu(x, w_ffn_in[l], w_ffn_out[l]), ln2_g[l], ln2_b[l])
    return x


def setup_inputs(seed: int = 0) -> dict:
    key = jax.random.key(seed)
    ks = jax.random.split(key, 16)
    f32 = jnp.float32
    nrm = lambda k, s: jax.random.normal(k, s, f32)
    return {
        "x_prompt": nrm(ks[0], (BATCH, SEQ, D_MODEL)),
        "x_sample": nrm(ks[1], (DEC_BATCH, DEC_SEQ, D_MODEL)),
        "w_in": nrm(ks[2], (DEPTH, D_MODEL, PROJ_W)) * D_MODEL ** -0.5,
        "gla_gate_w2": nrm(ks[3], (DEPTH, 2, GATE_RANK, GLA_QK_W)) * GATE_RANK ** -0.5,
        "gla_gate_b": nrm(ks[4], (DEPTH, 2, GLA_QK_W)) * 0.1,
        "gla_norm_g": 1.0 + 0.01 * nrm(ks[5], (DEPTH, GLA_V_W)),
        "nat_rpb": 0.02 * nrm(ks[6], (DEPTH, NAT_HEADS, 2 * MAX_KR - 1, 2 * KC - 1)),
        "w_out": nrm(ks[7], (DEPTH, MIX_W, D_MODEL)) * (MIX_W ** -0.5) * BETA,
        "ln1_g": 1.0 + 0.01 * nrm(ks[8], (DEPTH, D_MODEL)),
        "ln1_b": 0.01 * nrm(ks[9], (DEPTH, D_MODEL)),
        "w_ffn_in": nrm(ks[10], (DEPTH, D_MODEL, 2 * FFN_HIDDEN)) * D_MODEL ** -0.5,
        "w_ffn_out": nrm(ks[11], (DEPTH, FFN_HIDDEN, D_MODEL)) * (FFN_HIDDEN ** -0.5) * BETA,
        "ln2_g": 1.0 + 0.01 * nrm(ks[12], (DEPTH, D_MODEL)),
        "ln2_b": 0.01 * nrm(ks[13], (DEPTH, D_MODEL)),
    }


def reference(x_prompt, x_sample, w_in, gla_gate_w2, gla_gate_b, gla_norm_g, nat_rpb, w_out,
              ln1_g, ln1_b, w_ffn_in, w_ffn_out, ln2_g, ln2_b):
    y_prompt = _trunk(x_prompt, w_in, gla_gate_w2, gla_gate_b, gla_norm_g, nat_rpb, w_out,
                      ln1_g, ln1_b, w_ffn_in, w_ffn_out, ln2_g, ln2_b)
    y_sample = _trunk(x_sample, w_in, gla_gate_w2, gla_gate_b, gla_norm_g, nat_rpb, w_out,
                      ln1_g, ln1_b, w_ffn_in, w_ffn_out, ln2_g, ln2_b)
    return (y_prompt, y_sample)
```

```python
import functools

import numpy as np
import jax
import jax.numpy as jnp
from jax import lax
from jax.experimental import pallas as pl
from jax.experimental.pallas import tpu as pltpu

D_MODEL = 1024
DEPTH = 2
GLA_HEADS = 4
GLA_DK = 64
GLA_DV = 128
GLA_QK_W = GLA_HEADS * GLA_DK
GLA_V_W = GLA_HEADS * GLA_DV
GATE_RANK = 16
GATE_TAU = 16.0
CHUNK = 64
NAT_HEADS = 8
NAT_HD = 64
NAT_W = NAT_HEADS * NAT_HD
GRID_W = 64
MAX_KR = 8
KC = 16
FFN_HIDDEN = 2816
ALPHA = (2 * DEPTH) ** 0.25
LN_EPS = 1e-5
MASK_VALUE = -1e30

LANES = 128
LR_PAD = LANES
FB_W = GLA_HEADS * 2 * GLA_DK
QUAD = 4
QUAD_W = QUAD * NAT_HD
N_QUADS = NAT_HEADS // QUAD
WIN_KEYS = MAX_KR * GRID_W

VMEM_LIMIT = 56 * 1024 * 1024

PROJ_OUT_W = (2 * GLA_QK_W, GLA_V_W, GLA_V_W, NAT_W, NAT_W, NAT_W, LR_PAD)
PROJ_OUT_DT = (jnp.float32, jnp.bfloat16, jnp.float32, jnp.bfloat16, jnp.bfloat16,
               jnp.bfloat16, jnp.float32)


def _cparams(n_grid):
    return pltpu.CompilerParams(
        dimension_semantics=("arbitrary",) * n_grid, vmem_limit_bytes=VMEM_LIMIT)


def _inproj_kernel(x_ref, w_ref, *o_refs):
    xb = x_ref[...].astype(jnp.bfloat16)
    off = 0
    for o_ref, width in zip(o_refs, PROJ_OUT_W):
        acc = jnp.dot(xb, w_ref[:, off:off + width], preferred_element_type=jnp.float32)
        o_ref[...] = acc.astype(o_ref.dtype)
        off += width


def _inproj(x2d, w_perm, tm):
    t = x2d.shape[0]
    wtot = w_perm.shape[1]
    return pl.pallas_call(
        _inproj_kernel,
        out_shape=[jax.ShapeDtypeStruct((t, w), dt) for w, dt in zip(PROJ_OUT_W, PROJ_OUT_DT)],
        grid=(t // tm,),
        in_specs=[pl.BlockSpec((tm, D_MODEL), lambda i: (i, 0)),
                  pl.BlockSpec((D_MODEL, wtot), lambda i: (0, 0))],
        out_specs=[pl.BlockSpec((tm, w), lambda i: (i, 0)) for w in PROJ_OUT_W],
        compiler_params=_cparams(1),
        name="inproj",
    )(x2d, w_perm)


def _fb_expand(x):
    lane = lax.broadcasted_iota(jnp.int32, (x.shape[0], LANES), 1)
    low = lane < GLA_DK
    slabs = []
    for p in range(GLA_HEADS // 2):
        pair = x[:, p * LANES:(p + 1) * LANES]
        swapped = pltpu.roll(pair, GLA_DK, axis=1)
        slabs.append(jnp.where(low, pair, swapped))
        slabs.append(jnp.where(low, swapped, pair))
    return jnp.concatenate(slabs, axis=1)


def _gla_kernel(qk_ref, v_ref, r_ref, lr_ref, w2_ref, b2_ref, ng_ref, o_ref,
                bc_s, stf_s, st_s, *, n_chunks):
    f32, bf16 = jnp.float32, jnp.bfloat16
    lane_fb = lax.broadcasted_iota(jnp.int32, (1, FB_W), 1)
    fwd_lane = (lane_fb % LANES) < GLA_DK
    fwd_lane_h = lax.broadcasted_iota(jnp.int32, (1, LANES), 1) < GLA_DK
    row = lax.broadcasted_iota(jnp.int32, (CHUNK, CHUNK), 0)
    col = lax.broadcasted_iota(jnp.int32, (CHUNK, CHUNK), 1)
    tril = row >= col
    tri3 = jnp.concatenate([tril.astype(bf16)] * 3, axis=1)

    def chunk_rows(c):
        return pl.ds(pl.multiple_of(c * CHUNK, CHUNK), CHUNK)

    def state_update(c_rows, ke, dec):
        for h in range(GLA_HEADS):
            sl = slice(h * LANES, (h + 1) * LANES)
            d_st = lax.dot_general(v_ref[c_rows, sl], ke[:, sl], (((0,), (0,)), ((), ())),
                                   preferred_element_type=f32)
            st_s[h] = st_s[h] * dec[:, sl] + d_st

    st_s[...] = jnp.zeros_like(st_s)

    def pass1(c, carry):
        c_rows = chunk_rows(c)
        z = jnp.dot(lr_ref[c_rows, :].astype(bf16), w2_ref[...],
                    preferred_element_type=f32) + b2_ref[...]
        g = (jnp.minimum(z, 0.0) - jnp.log1p(jnp.exp(-jnp.abs(z)))) * (1.0 / GATE_TAU)
        g1 = g.astype(bf16)
        res = g - g1.astype(f32)
        g2 = res.astype(bf16)
        g3 = (res - g2.astype(f32)).astype(bf16)
        pre = jnp.dot(tri3, jnp.concatenate([g1, g2, g3], axis=0), preferred_element_type=f32)
        tot = pre[CHUNK - 1:CHUNK, :]
        bc = jnp.where(fwd_lane, pre, tot - pre + g)
        bc_s[c_rows, :] = bc
        kk = _fb_expand(qk_ref[c_rows, GLA_QK_W:])
        ke = (kk * jnp.exp(tot - bc)).astype(bf16)
        for h in range(GLA_HEADS):
            stf_s[c * GLA_HEADS + h] = st_s[h].astype(bf16)
        state_update(c_rows, ke, jnp.exp(tot))
        return carry

    lax.fori_loop(0, n_chunks, pass1, 0)

    st_s[...] = jnp.zeros_like(st_s)

    def pass2(i, carry):
        c = n_chunks - 1 - i
        c_rows = chunk_rows(c)
        bc = bc_s[c_rows, :]
        tot = jnp.where(fwd_lane, bc[CHUNK - 1:CHUNK, :], bc[0:1, :])
        qq = _fb_expand(qk_ref[c_rows, :GLA_QK_W])
        kk = _fb_expand(qk_ref[c_rows, GLA_QK_W:])
        qd = (qq * jnp.exp(bc)).astype(bf16)
        ki = (kk * jnp.exp(-bc)).astype(bf16)
        ke = (kk * jnp.exp(tot - bc)).astype(bf16)
        zero = jnp.zeros((CHUNK, LANES), bf16)
        for h in range(GLA_HEADS):
            sl = slice(h * LANES, (h + 1) * LANES)
            qd_h = qd[:, sl]
            lhs = jnp.concatenate([jnp.where(fwd_lane_h, qd_h, zero),
                                   jnp.where(fwd_lane_h, zero, qd_h)], axis=0)
            sc = lax.dot_general(lhs, ki[:, sl], (((1,), (1,)), ((), ())),
                                 preferred_element_type=f32)
            a = jnp.where(tril, sc[:CHUNK], sc[CHUNK:]).astype(bf16)
            v_h = v_ref[c_rows, sl]
            st_c = jnp.where(fwd_lane_h, stf_s[c * GLA_HEADS + h], st_s[h].astype(bf16))
            o_h = jnp.dot(a, v_h, preferred_element_type=f32)
            o_h = o_h + lax.dot_general(qd_h, st_c, (((1,), (1,)), ((), ())),
                                        preferred_element_type=f32)
            o_h = o_h * lax.rsqrt(jnp.mean(o_h * o_h, axis=-1, keepdims=True) + LN_EPS)
            r_h = r_ref[c_rows, sl]
            gate = r_h * (1.0 / (1.0 + jnp.exp(-r_h)))
            o_ref[c_rows, sl] = (o_h * ng_ref[:, sl] * gate).astype(o_ref.dtype)
        state_update(c_rows, ke, jnp.exp(tot))
        return carry

    lax.fori_loop(0, n_chunks, pass2, 0)


def _gla(qk, v, r, lr, w2p, b2p, ng):
    b, l, _ = qk.shape
    n_chunks = l // CHUNK
    seq = lambda w: pl.BlockSpec((None, l, w), lambda i: (i, 0, 0))
    const = lambda s: pl.BlockSpec(s, lambda i: (0,) * len(s))
    return pl.pallas_call(
        functools.partial(_gla_kernel, n_chunks=n_chunks),
        out_shape=jax.ShapeDtypeStruct((b, l, GLA_V_W), jnp.bfloat16),
        grid=(b,),
        in_specs=[seq(2 * GLA_QK_W), seq(GLA_V_W), seq(GLA_V_W), seq(LR_PAD),
                  const((LR_PAD, FB_W)), const((1, FB_W)), const((1, GLA_V_W))],
        out_specs=seq(GLA_V_W),
        scratch_shapes=[pltpu.VMEM((l, FB_W), jnp.float32),
                        pltpu.VMEM((n_chunks * GLA_HEADS, GLA_DV, LANES), jnp.bfloat16),
                        pltpu.VMEM((GLA_HEADS, GLA_DV, LANES), jnp.float32)],
        compiler_params=_cparams(1),
        name="gla",
    )(qk, v, r, lr, w2p, b2p, ng)


def _nat_kernel(q_ref, k_ref, v_ref, bias_ref, o_ref, *, n_rows):
    f32, bf16 = jnp.float32, jnp.bfloat16
    blk_r = lax.broadcasted_iota(jnp.int32, (QUAD_W, QUAD_W), 0) // NAT_HD
    blk_c = lax.broadcasted_iota(jnp.int32, (QUAD_W, QUAD_W), 1) // NAT_HD
    blockdiag = blk_r == blk_c
    out_head = lax.broadcasted_iota(jnp.int32, (GRID_W, QUAD_W), 1) // NAT_HD

    def row_body(r, carry):
        rs = jnp.clip(r - MAX_KR // 2, 0, n_rows - MAX_KR)
        e = r - rs
        q_rows = pl.ds(pl.multiple_of(r * GRID_W, GRID_W), GRID_W)
        w_rows = pl.ds(pl.multiple_of(rs * GRID_W, GRID_W), WIN_KEYS)
        for quad in range(N_QUADS):
            sl = slice(quad * QUAD_W, (quad + 1) * QUAD_W)
            q_r = q_ref[q_rows, sl]
            q_bd = jnp.where(blockdiag, jnp.concatenate([q_r] * QUAD, axis=0),
                             jnp.zeros((QUAD_W, QUAD_W), bf16))
            s = lax.dot_general(q_bd, k_ref[w_rows, sl], (((1,), (1,)), ((), ())),
                                preferred_element_type=f32)
            s = s + bias_ref[e, quad]
            m = jnp.max(s, axis=-1, keepdims=True)
            p = jnp.exp(s - m)
            denom = jnp.sum(p, axis=-1, keepdims=True)
            o = jnp.dot(p.astype(bf16), v_ref[w_rows, sl], preferred_element_type=f32)
            o = o * (1.0 / denom)
            acc = jnp.zeros((GRID_W, QUAD_W), f32)
            for h in range(QUAD):
                acc = acc + jnp.where(out_head == h, o[h * GRID_W:(h + 1) * GRID_W, :], 0.0)
            o_ref[q_rows, sl] = acc.astype(o_ref.dtype)
        return carry

    lax.fori_loop(0, n_rows, row_body, 0)


def _nat(q, k, v, bias):
    b, l, _ = q.shape
    n_rows = l // GRID_W
    seq = pl.BlockSpec((None, l, NAT_W), lambda i: (i, 0, 0))
    return pl.pallas_call(
        functools.partial(_nat_kernel, n_rows=n_rows),
        out_shape=jax.ShapeDtypeStruct((b, l, NAT_W), jnp.bfloat16),
        grid=(b,),
        in_specs=[seq, seq, seq,
                  pl.BlockSpec(bias.shape, lambda i: (0, 0, 0, 0))],
        out_specs=seq,
        compiler_params=_cparams(1),
        name="nat",
    )(q, k, v, bias)


def _nat_bias_table(rpb):
    c = np.arange(GRID_W)[:, None]
    m = np.arange(GRID_W)[None, :]
    col_start = np.clip(c - KC // 2, 0, GRID_W - KC)
    valid = (m >= col_start) & (m < col_start + KC)
    co = np.clip(m - c + KC - 1, 0, 2 * KC - 2)
    e = np.arange(MAX_KR)[:, None]
    j = np.arange(MAX_KR)[None, :]
    ro = j - e + MAX_KR - 1
    t = rpb[:, ro][:, :, :, co]
    t = jnp.where(jnp.asarray(valid)[None, None, None], t, MASK_VALUE)
    t = jnp.transpose(t, (1, 0, 3, 2, 4))
    return t.reshape(MAX_KR, N_QUADS, QUAD * GRID_W, MAX_KR * GRID_W).astype(jnp.float32)


def _layer_norm(y, g, b):
    mu = jnp.mean(y, axis=-1, keepdims=True)
    yc = y - mu
    var = jnp.mean(yc * yc, axis=-1, keepdims=True)
    return yc * lax.rsqrt(var + LN_EPS) * g + b


def _outproj_kernel(oa_ref, on_ref, x_ref, wa_ref, wn_ref, g_ref, b_ref, o_ref):
    mix = jnp.dot(oa_ref[...], wa_ref[...], preferred_element_type=jnp.float32)
    mix = mix + jnp.dot(on_ref[...], wn_ref[...], preferred_element_type=jnp.float32)
    o_ref[...] = _layer_norm(ALPHA * x_ref[...] + mix, g_ref[...], b_ref[...])


def _outproj(oa, on, x2d, wa, wn, g, b, tm):
    t = x2d.shape[0]
    tile = lambda w: pl.BlockSpec((tm, w), lambda i: (i, 0))
    const = lambda s: pl.BlockSpec(s, lambda i: (0, 0))
    return pl.pallas_call(
        _outproj_kernel,
        out_shape=jax.ShapeDtypeStruct((t, D_MODEL), jnp.float32),
        grid=(t // tm,),
        in_specs=[tile(GLA_V_W), tile(NAT_W), tile(D_MODEL),
                  const((GLA_V_W, D_MODEL)), const((NAT_W, D_MODEL)),
                  const((1, D_MODEL)), const((1, D_MODEL))],
        out_specs=tile(D_MODEL),
        compiler_params=_cparams(1),
        name="outproj_ln",
    )(oa, on, x2d, wa, wn, g, b)


def _ffn_kernel(x_ref, w1_ref, w2_ref, g_ref, b_ref, o_ref):
    x = x_ref[...]
    h = jnp.dot(x.astype(jnp.bfloat16), w1_ref[...], preferred_element_type=jnp.float32)
    gate = h[:, :FFN_HIDDEN]
    act = gate * (1.0 / (1.0 + jnp.exp(-gate))) * h[:, FFN_HIDDEN:]
    y = jnp.dot(act.astype(jnp.bfloat16), w2_ref[...], preferred_element_type=jnp.float32)
    o_ref[...] = _layer_norm(ALPHA * x + y, g_ref[...], b_ref[...])


def _ffn(x2d, w1, w2, g, b, tm):
    t = x2d.shape[0]
    tile = pl.BlockSpec((tm, D_MODEL), lambda i: (i, 0))
    const = lambda s: pl.BlockSpec(s, lambda i: (0, 0), pipeline_mode=pl.Buffered(1))
    return pl.pallas_call(
        _ffn_kernel,
        out_shape=jax.ShapeDtypeStruct((t, D_MODEL), jnp.float32),
        grid=(t // tm,),
        in_specs=[tile, const((D_MODEL, 2 * FFN_HIDDEN)), const((FFN_HIDDEN, D_MODEL)),
                  const((1, D_MODEL)), const((1, D_MODEL))],
        out_specs=tile,
        compiler_params=_cparams(1),
        name="ffn_ln",
    )(x2d, w1, w2, g, b)


def _prep_layer(w_in, gate_w2, gate_b, norm_g, rpb, w_out, ln1_g, ln1_b,
                w_ffn_in, w_ffn_out, ln2_g, ln2_b):
    f32, bf16 = jnp.float32, jnp.bfloat16
    sizes = (GLA_QK_W, GLA_QK_W, GLA_V_W, GLA_V_W, 2 * GATE_RANK, NAT_W, NAT_W, NAT_W)
    pts = np.cumsum((0,) + sizes)
    q_a, k_a, v_a, r_a, lr, q_b, k_b, v_b = [w_in[:, pts[i]:pts[i + 1]] for i in range(8)]
    lr = jnp.pad(lr, ((0, 0), (0, LR_PAD - 2 * GATE_RANK)))
    w_perm = jnp.concatenate(
        [q_a * GLA_DK ** -0.5, k_a, v_a, r_a, q_b * NAT_HD ** -0.5, k_b, v_b, lr], axis=1)
    w2p = jnp.zeros((LR_PAD, GLA_HEADS, 2, GLA_DK), f32)
    w2p = w2p.at[:GATE_RANK, :, 0, :].set(gate_w2[0].reshape(GATE_RANK, GLA_HEADS, GLA_DK))
    w2p = w2p.at[GATE_RANK:2 * GATE_RANK, :, 1, :].set(
        gate_w2[1].reshape(GATE_RANK, GLA_HEADS, GLA_DK))
    b2p = jnp.stack([gate_b[0].reshape(GLA_HEADS, GLA_DK),
                     gate_b[1].reshape(GLA_HEADS, GLA_DK)], axis=1)
    return dict(
        w_perm=w_perm.astype(bf16),
        w2p=w2p.reshape(LR_PAD, FB_W).astype(bf16),
        b2p=b2p.reshape(1, FB_W).astype(f32),
        ng=norm_g.reshape(1, GLA_V_W).astype(f32),
        bias=_nat_bias_table(rpb.astype(f32)),
        wa=w_out[:GLA_V_W].astype(bf16), wn=w_out[GLA_V_W:].astype(bf16),
        ln1_g=ln1_g.reshape(1, D_MODEL), ln1_b=ln1_b.reshape(1, D_MODEL),
        w1=w_ffn_in.astype(bf16), w2=w_ffn_out.astype(bf16),
        ln2_g=ln2_g.reshape(1, D_MODEL), ln2_b=ln2_b.reshape(1, D_MODEL),
    )


def _row_tile(t):
    return 512 if t % 512 == 0 else t


def _layer(x2d, b, l, p):
    tm = _row_tile(x2d.shape[0])
    qk, v_a, r_a, q_b, k_b, v_b, lr = _inproj(x2d, p["w_perm"], tm)
    seq = lambda a: a.reshape(b, l, a.shape[-1])
    o_a = _gla(seq(qk), seq(v_a), seq(r_a), seq(lr), p["w2p"], p["b2p"], p["ng"])
    o_n = _nat(seq(q_b), seq(k_b), seq(v_b), p["bias"])
    x1 = _outproj(o_a.reshape(b * l, GLA_V_W), o_n.reshape(b * l, NAT_W), x2d,
                  p["wa"], p["wn"], p["ln1_g"], p["ln1_b"], tm)
    return _ffn(x1, p["w1"], p["w2"], p["ln2_g"], p["ln2_b"], tm)


def _trunk(x, layers):
    b, l, d = x.shape
    x2d = x.reshape(b * l, d)
    for p in layers:
        x2d = _layer(x2d, b, l, p)
    return x2d.reshape(b, l, d)


def kernel(x_prompt, x_sample, w_in, gla_gate_w2, gla_gate_b, gla_norm_g, nat_rpb, w_out,
           ln1_g, ln1_b, w_ffn_in, w_ffn_out, ln2_g, ln2_b):
    layers = [_prep_layer(w_in[i], gla_gate_w2[i], gla_gate_b[i], gla_norm_g[i], nat_rpb[i],
                          w_out[i], ln1_g[i], ln1_b[i], w_ffn_in[i], w_ffn_out[i],
                          ln2_g[i], ln2_b[i]) for i in range(DEPTH)]
    return (_trunk(x_prompt, layers), _trunk(x_sample, layers))
```

```python
import functools

import numpy as np
import jax
import jax.numpy as jnp
from jax import lax
from jax.experimental import pallas as pl
from jax.experimental.pallas import tpu as pltpu

D_MODEL = 1024
DEPTH = 2
GLA_HEADS = 4
GLA_DK = 64
GLA_DV = 128
GLA_QK_W = GLA_HEADS * GLA_DK
GLA_V_W = GLA_HEADS * GLA_DV
GATE_RANK = 16
GATE_TAU = 16.0
CHUNK = 64
NAT_HEADS = 8
NAT_HD = 64
NAT_W = NAT_HEADS * NAT_HD
GRID_W = 64
MAX_KR = 8
KC = 16
FFN_HIDDEN = 2816
ALPHA = (2 * DEPTH) ** 0.25
LN_EPS = 1e-5
MASK_VALUE = -1e30

LANES = 128
LR_PAD = LANES
FB_W = GLA_HEADS * 2 * GLA_DK
QUAD = 4
QUAD_W = QUAD * NAT_HD
N_QUADS = NAT_HEADS // QUAD
WIN_KEYS = MAX_KR * GRID_W
GLA_BLOCK_CHUNKS = 4

VMEM_LIMIT = 56 * 1024 * 1024

PROJ_OUT_W = (2 * GLA_QK_W, GLA_V_W, GLA_V_W, NAT_W, NAT_W, NAT_W, LR_PAD)
PROJ_OUT_DT = (jnp.float32, jnp.bfloat16, jnp.float32, jnp.bfloat16, jnp.bfloat16,
               jnp.bfloat16, jnp.float32)


def _cparams(n_grid):
    return pltpu.CompilerParams(
        dimension_semantics=("arbitrary",) * n_grid, vmem_limit_bytes=VMEM_LIMIT)


def _inproj_kernel(x_ref, w_ref, *o_refs):
    xb = x_ref[...].astype(jnp.bfloat16)
    off = 0
    for o_ref, width in zip(o_refs, PROJ_OUT_W):
        acc = jnp.dot(xb, w_ref[:, off:off + width], preferred_element_type=jnp.float32)
        o_ref[...] = acc.astype(o_ref.dtype)
        off += width


def _inproj(x2d, w_perm, tm):
    t = x2d.shape[0]
    wtot = w_perm.shape[1]
    return pl.pallas_call(
        _inproj_kernel,
        out_shape=[jax.ShapeDtypeStruct((t, w), dt) for w, dt in zip(PROJ_OUT_W, PROJ_OUT_DT)],
        grid=(t // tm,),
        in_specs=[pl.BlockSpec((tm, D_MODEL), lambda i: (i, 0)),
                  pl.BlockSpec((D_MODEL, wtot), lambda i: (0, 0))],
        out_specs=[pl.BlockSpec((tm, w), lambda i: (i, 0)) for w in PROJ_OUT_W],
        compiler_params=_cparams(1),
        name="inproj",
    )(x2d, w_perm)


def _fb_expand(x):
    lane = lax.broadcasted_iota(jnp.int32, (x.shape[0], LANES), 1)
    low = lane < GLA_DK
    slabs = []
    for p in range(GLA_HEADS // 2):
        pair = x[:, p * LANES:(p + 1) * LANES]
        swapped = pltpu.roll(pair, GLA_DK, axis=1)
        slabs.append(jnp.where(low, pair, swapped))
        slabs.append(jnp.where(low, swapped, pair))
    return jnp.concatenate(slabs, axis=1)


def _gla_kernel(qk_ref, v_ref, r_ref, lr_ref, w2_ref, b2_ref, ng_ref, o_ref,
                qd_s, ki_s, dec_s, dst_s, stf_s, st_s, *, n_chunks):
    f32, bf16 = jnp.float32, jnp.bfloat16
    nb = GLA_BLOCK_CHUNKS
    n_blocks = n_chunks // nb
    blk = nb * CHUNK
    lane_fb = lax.broadcasted_iota(jnp.int32, (1, FB_W), 1)
    fwd_lane = (lane_fb % LANES) < GLA_DK
    fwd_lane_h = lax.broadcasted_iota(jnp.int32, (1, LANES), 1) < GLA_DK
    row = lax.broadcasted_iota(jnp.int32, (CHUNK, CHUNK), 0)
    col = lax.broadcasted_iota(jnp.int32, (CHUNK, CHUNK), 1)
    tril = row >= col
    tri3 = jnp.concatenate([tril.astype(bf16)] * 3, axis=1)
    heads = [slice(h * LANES, (h + 1) * LANES) for h in range(GLA_HEADS)]
    chunks = [slice(j * CHUNK, (j + 1) * CHUNK) for j in range(nb)]

    st_s[...] = jnp.zeros_like(st_s)

    def pass1(b, carry):
        rows = pl.ds(pl.multiple_of(b * blk, blk), blk)
        c0 = b * nb
        z = jnp.dot(lr_ref[rows, :].astype(bf16), w2_ref[...],
                    preferred_element_type=f32) + b2_ref[...]
        g = (jnp.minimum(z, 0.0) - jnp.log(1.0 + jnp.exp(-jnp.abs(z)))) * (1.0 / GATE_TAU)
        g1 = g.astype(bf16)
        res = g - g1.astype(f32)
        g2 = res.astype(bf16)
        g3 = (res - g2.astype(f32)).astype(bf16)
        pre = [jnp.dot(tri3, jnp.concatenate([g1[cs], g2[cs], g3[cs]], axis=0),
                       preferred_element_type=f32) for cs in chunks]
        qq = _fb_expand(qk_ref[rows, :GLA_QK_W])
        kk = _fb_expand(qk_ref[rows, GLA_QK_W:])
        ke, dec = [], []
        for j, cs in enumerate(chunks):
            tot = pre[j][CHUNK - 1:CHUNK, :]
            bc = jnp.where(fwd_lane, pre[j], tot - pre[j] + g[cs])
            dec.append(jnp.exp(tot))
            dec_s[c0 + j] = dec[j]
            c_rows = pl.ds(pl.multiple_of((c0 + j) * CHUNK, CHUNK), CHUNK)
            qd_s[c_rows, :] = (qq[cs] * jnp.exp(bc)).astype(bf16)
            ki_s[c_rows, :] = (kk[cs] * jnp.exp(-bc)).astype(bf16)
            ke.append((kk[cs] * jnp.exp(tot - bc)).astype(bf16))
        v_blk = v_ref[rows, :]
        d_st = [[lax.dot_general(v_blk[cs, sl], ke[j][:, sl], (((0,), (0,)), ((), ())),
                                 preferred_element_type=f32)
                 for sl in heads] for j, cs in enumerate(chunks)]
        for h, sl in enumerate(heads):
            st = st_s[h]
            for j in range(nb):
                dst_s[(c0 + j) * GLA_HEADS + h] = d_st[j][h]
                stf_s[(c0 + j) * GLA_HEADS + h] = st.astype(bf16)
                st = st * dec[j][:, sl] + d_st[j][h]
            st_s[h] = st
        return carry

    lax.fori_loop(0, n_blocks, pass1, 0)

    st_s[...] = jnp.zeros_like(st_s)

    def pass2(i, carry):
        b = n_blocks - 1 - i
        rows = pl.ds(pl.multiple_of(b * blk, blk), blk)
        c0 = b * nb
        qd = qd_s[rows, :]
        ki = ki_s[rows, :]
        v_blk = v_ref[rows, :]
        st_c = [[None] * GLA_HEADS for _ in range(nb)]
        for h, sl in enumerate(heads):
            st = st_s[h]
            for j in reversed(range(nb)):
                idx = (c0 + j) * GLA_HEADS + h
                st_c[j][h] = jnp.where(fwd_lane_h, stf_s[idx], st.astype(bf16))
                st = st * dec_s[c0 + j][:, sl] + dst_s[idx]
            st_s[h] = st
        zero = jnp.zeros((CHUNK, LANES), bf16)
        sc = []
        for cs in chunks:
            for sl in heads:
                qd_h = qd[cs, sl]
                lhs = jnp.concatenate([jnp.where(fwd_lane_h, qd_h, zero),
                                       jnp.where(fwd_lane_h, zero, qd_h)], axis=0)
                sc.append(lax.dot_general(lhs, ki[cs, sl], (((1,), (1,)), ((), ())),
                                          preferred_element_type=f32))
        outs = []
        for j, cs in enumerate(chunks):
            for h, sl in enumerate(heads):
                s = sc[j * GLA_HEADS + h]
                a = jnp.where(tril, s[:CHUNK], s[CHUNK:]).astype(bf16)
                o_h = jnp.dot(a, v_blk[cs, sl], preferred_element_type=f32)
                o_h = o_h + lax.dot_general(qd[cs, sl], st_c[j][h], (((1,), (1,)), ((), ())),
                                            preferred_element_type=f32)
                outs.append(o_h)
        r_blk = r_ref[rows, :]
        gate = r_blk * (1.0 / (1.0 + jnp.exp(-r_blk))) * ng_ref[...]
        for j, cs in enumerate(chunks):
            for h, sl in enumerate(heads):
                o_h = outs[j * GLA_HEADS + h]
                o_h = o_h * lax.rsqrt(jnp.mean(o_h * o_h, axis=-1, keepdims=True) + LN_EPS)
                c_rows = pl.ds(pl.multiple_of((c0 + j) * CHUNK, CHUNK), CHUNK)
                o_ref[c_rows, sl] = (o_h * gate[cs, sl]).astype(o_ref.dtype)
        return carry

    lax.fori_loop(0, n_blocks, pass2, 0)


def _gla(qk, v, r, lr, w2p, b2p, ng):
    b, l, _ = qk.shape
    n_chunks = l // CHUNK
    seq = lambda w: pl.BlockSpec((None, l, w), lambda i: (i, 0, 0))
    const = lambda s: pl.BlockSpec(s, lambda i: (0,) * len(s))
    return pl.pallas_call(
        functools.partial(_gla_kernel, n_chunks=n_chunks),
        out_shape=jax.ShapeDtypeStruct((b, l, GLA_V_W), jnp.bfloat16),
        grid=(b,),
        in_specs=[seq(2 * GLA_QK_W), seq(GLA_V_W), seq(GLA_V_W), seq(LR_PAD),
                  const((LR_PAD, FB_W)), const((1, FB_W)), const((1, GLA_V_W))],
        out_specs=seq(GLA_V_W),
        scratch_shapes=[pltpu.VMEM((l, FB_W), jnp.bfloat16),
                        pltpu.VMEM((l, FB_W), jnp.bfloat16),
                        pltpu.VMEM((n_chunks, 1, FB_W), jnp.float32),
                        pltpu.VMEM((n_chunks * GLA_HEADS, GLA_DV, LANES), jnp.float32),
                        pltpu.VMEM((n_chunks * GLA_HEADS, GLA_DV, LANES), jnp.bfloat16),
                        pltpu.VMEM((GLA_HEADS, GLA_DV, LANES), jnp.float32)],
        compiler_params=_cparams(1),
        name="gla",
    )(qk, v, r, lr, w2p, b2p, ng)


def _nat_kernel(q_ref, k_ref, v_ref, bias_ref, o_ref, *, n_rows):
    f32, bf16 = jnp.float32, jnp.bfloat16
    blk_r = lax.broadcasted_iota(jnp.int32, (QUAD_W, QUAD_W), 0) // NAT_HD
    blk_c = lax.broadcasted_iota(jnp.int32, (QUAD_W, QUAD_W), 1) // NAT_HD
    blockdiag = blk_r == blk_c
    out_head = lax.broadcasted_iota(jnp.int32, (GRID_W, QUAD_W), 1) // NAT_HD

    def row_body(r, carry):
        rs = jnp.clip(r - MAX_KR // 2, 0, n_rows - MAX_KR)
        e = r - rs
        q_rows = pl.ds(pl.multiple_of(r * GRID_W, GRID_W), GRID_W)
        w_rows = pl.ds(pl.multiple_of(rs * GRID_W, GRID_W), WIN_KEYS)
        for quad in range(N_QUADS):
            sl = slice(quad * QUAD_W, (quad + 1) * QUAD_W)
            q_r = q_ref[q_rows, sl]
            q_bd = jnp.where(blockdiag, jnp.concatenate([q_r] * QUAD, axis=0),
                             jnp.zeros((QUAD_W, QUAD_W), bf16))
            s = lax.dot_general(q_bd, k_ref[w_rows, sl], (((1,), (1,)), ((), ())),
                                preferred_element_type=f32)
            s = s + bias_ref[e, quad]
            m = jnp.max(s, axis=-1, keepdims=True)
            p = jnp.exp(s - m)
            denom = jnp.sum(p, axis=-1, keepdims=True)
            o = jnp.dot(p.astype(bf16), v_ref[w_rows, sl], preferred_element_type=f32)
            o = o * (1.0 / denom)
            acc = jnp.zeros((GRID_W, QUAD_W), f32)
            for h in range(QUAD):
                acc = acc + jnp.where(out_head == h, o[h * GRID_W:(h + 1) * GRID_W, :], 0.0)
            o_ref[q_rows, sl] = acc.astype(o_ref.dtype)
        return carry

    lax.fori_loop(0, n_rows, row_body, 0)


def _nat(q, k, v, bias):
    b, l, _ = q.shape
    n_rows = l // GRID_W
    seq = pl.BlockSpec((None, l, NAT_W), lambda i: (i, 0, 0))
    return pl.pallas_call(
        functools.partial(_nat_kernel, n_rows=n_rows),
        out_shape=jax.ShapeDtypeStruct((b, l, NAT_W), jnp.bfloat16),
        grid=(b,),
        in_specs=[seq, seq, seq,
                  pl.BlockSpec(bias.shape, lambda i: (0, 0, 0, 0))],
        out_specs=seq,
        compiler_params=_cparams(1),
        name="nat",
    )(q, k, v, bias)


def _nat_bias_table(rpb):
    c = np.arange(GRID_W)[:, None]
    m = np.arange(GRID_W)[None, :]
    col_start = np.clip(c - KC // 2, 0, GRID_W - KC)
    valid = (m >= col_start) & (m < col_start + KC)
    co = np.clip(m - c + KC - 1, 0, 2 * KC - 2)
    e = np.arange(MAX_KR)[:, None]
    j = np.arange(MAX_KR)[None, :]
    ro = j - e + MAX_KR - 1
    t = rpb[:, ro][:, :, :, co]
    t = jnp.where(jnp.asarray(valid)[None, None, None], t, MASK_VALUE)
    t = jnp.transpose(t, (1, 0, 3, 2, 4))
    return t.reshape(MAX_KR, N_QUADS, QUAD * GRID_W, MAX_KR * GRID_W).astype(jnp.float32)


def _layer_norm(y, g, b):
    mu = jnp.mean(y, axis=-1, keepdims=True)
    yc = y - mu
    var = jnp.mean(yc * yc, axis=-1, keepdims=True)
    return yc * lax.rsqrt(var + LN_EPS) * g + b


def _outproj_kernel(oa_ref, on_ref, x_ref, wa_ref, wn_ref, g_ref, b_ref, o_ref):
    mix = jnp.dot(oa_ref[...], wa_ref[...], preferred_element_type=jnp.float32)
    mix = mix + jnp.dot(on_ref[...], wn_ref[...], preferred_element_type=jnp.float32)
    o_ref[...] = _layer_norm(ALPHA * x_ref[...] + mix, g_ref[...], b_ref[...])


def _outproj(oa, on, x2d, wa, wn, g, b, tm):
    t = x2d.shape[0]
    tile = lambda w: pl.BlockSpec((tm, w), lambda i: (i, 0))
    const = lambda s: pl.BlockSpec(s, lambda i: (0, 0))
    return pl.pallas_call(
        _outproj_kernel,
        out_shape=jax.ShapeDtypeStruct((t, D_MODEL), jnp.float32),
        grid=(t // tm,),
        in_specs=[tile(GLA_V_W), tile(NAT_W), tile(D_MODEL),
                  const((GLA_V_W, D_MODEL)), const((NAT_W, D_MODEL)),
                  const((1, D_MODEL)), const((1, D_MODEL))],
        out_specs=tile(D_MODEL),
        compiler_params=_cparams(1),
        name="outproj_ln",
    )(oa, on, x2d, wa, wn, g, b)


def _ffn_kernel(x_ref, w1_ref, w2_ref, g_ref, b_ref, o_ref):
    x = x_ref[...]
    h = jnp.dot(x.astype(jnp.bfloat16), w1_ref[...], preferred_element_type=jnp.float32)
    gate = h[:, :FFN_HIDDEN]
    act = gate * (1.0 / (1.0 + jnp.exp(-gate))) * h[:, FFN_HIDDEN:]
    y = jnp.dot(act.astype(jnp.bfloat16), w2_ref[...], preferred_element_type=jnp.float32)
    o_ref[...] = _layer_norm(ALPHA * x + y, g_ref[...], b_ref[...])


def _ffn(x2d, w1, w2, g, b, tm):
    t = x2d.shape[0]
    tile = pl.BlockSpec((tm, D_MODEL), lambda i: (i, 0))
    const = lambda s: pl.BlockSpec(s, lambda i: (0, 0), pipeline_mode=pl.Buffered(1))
    return pl.pallas_call(
        _ffn_kernel,
        out_shape=jax.ShapeDtypeStruct((t, D_MODEL), jnp.float32),
        grid=(t // tm,),
        in_specs=[tile, const((D_MODEL, 2 * FFN_HIDDEN)), const((FFN_HIDDEN, D_MODEL)),
                  const((1, D_MODEL)), const((1, D_MODEL))],
        out_specs=tile,
        compiler_params=_cparams(1),
        name="ffn_ln",
    )(x2d, w1, w2, g, b)


def _prep_layer(w_in, gate_w2, gate_b, norm_g, rpb, w_out, ln1_g, ln1_b,
                w_ffn_in, w_ffn_out, ln2_g, ln2_b):
    f32, bf16 = jnp.float32, jnp.bfloat16
    sizes = (GLA_QK_W, GLA_QK_W, GLA_V_W, GLA_V_W, 2 * GATE_RANK, NAT_W, NAT_W, NAT_W)
    pts = np.cumsum((0,) + sizes)
    q_a, k_a, v_a, r_a, lr, q_b, k_b, v_b = [w_in[:, pts[i]:pts[i + 1]] for i in range(8)]
    lr = jnp.pad(lr, ((0, 0), (0, LR_PAD - 2 * GATE_RANK)))
    w_perm = jnp.concatenate(
        [q_a * GLA_DK ** -0.5, k_a, v_a, r_a, q_b * NAT_HD ** -0.5, k_b, v_b, lr], axis=1)
    w2p = jnp.zeros((LR_PAD, GLA_HEADS, 2, GLA_DK), f32)
    w2p = w2p.at[:GATE_RANK, :, 0, :].set(gate_w2[0].reshape(GATE_RANK, GLA_HEADS, GLA_DK))
    w2p = w2p.at[GATE_RANK:2 * GATE_RANK, :, 1, :].set(
        gate_w2[1].reshape(GATE_RANK, GLA_HEADS, GLA_DK))
    b2p = jnp.stack([gate_b[0].reshape(GLA_HEADS, GLA_DK),
                     gate_b[1].reshape(GLA_HEADS, GLA_DK)], axis=1)
    return dict(
        w_perm=w_perm.astype(bf16),
        w2p=w2p.reshape(LR_PAD, FB_W).astype(bf16),
        b2p=b2p.reshape(1, FB_W).astype(f32),
        ng=norm_g.reshape(1, GLA_V_W).astype(f32),
        bias=_nat_bias_table(rpb.astype(f32)),
        wa=w_out[:GLA_V_W].astype(bf16), wn=w_out[GLA_V_W:].astype(bf16),
        ln1_g=ln1_g.reshape(1, D_MODEL), ln1_b=ln1_b.reshape(1, D_MODEL),
        w1=w_ffn_in.astype(bf16), w2=w_ffn_out.astype(bf16),
        ln2_g=ln2_g.reshape(1, D_MODEL), ln2_b=ln2_b.reshape(1, D_MODEL),
    )


def _row_tile(t):
    return 512 if t % 512 == 0 else t


def _layer(x2d, b, l, p):
    tm = _row_tile(x2d.shape[0])
    qk, v_a, r_a, q_b, k_b, v_b, lr = _inproj(x2d, p["w_perm"], tm)
    seq = lambda a: a.reshape(b, l, a.shape[-1])
    o_a = _gla(seq(qk), seq(v_a), seq(r_a), seq(lr), p["w2p"], p["b2p"], p["ng"])
    o_n = _nat(seq(q_b), seq(k_b), seq(v_b), p["bias"])
    x1 = _outproj(o_a.reshape(b * l, GLA_V_W), o_n.reshape(b * l, NAT_W), x2d,
                  p["wa"], p["wn"], p["ln1_g"], p["ln1_b"], tm)
    return _ffn(x1, p["w1"], p["w2"], p["ln2_g"], p["ln2_b"], tm)


def _trunk(x, layers):
    b, l, d = x.shape
    x2d = x.reshape(b * l, d)
    for p in layers:
        x2d = _layer(x2d, b, l, p)
    return x2d.reshape(b, l, d)


def kernel(x_prompt, x_sample, w_in, gla_gate_w2, gla_gate_b, gla_norm_g, nat_rpb, w_out,
           ln1_g, ln1_b, w_ffn_in, w_ffn_out, ln2_g, ln2_b):
    layers = [_prep_layer(w_in[i], gla_gate_w2[i], gla_gate_b[i], gla_norm_g[i], nat_rpb[i],
                          w_out[i], ln1_g[i], ln1_b[i], w_ffn_in[i], w_ffn_out[i],
                          ln2_g[i], ln2_b[i]) for i in range(DEPTH)]
    return (_trunk(x_prompt, layers), _trunk(x_sample, layers))
```

```python
import functools

import numpy as np
import jax
import jax.numpy as jnp
from jax import lax
from jax.experimental import pallas as pl
from jax.experimental.pallas import tpu as pltpu

D_MODEL = 1024
DEPTH = 2
GLA_HEADS = 4
GLA_DK = 64
GLA_DV = 128
GLA_QK_W = GLA_HEADS * GLA_DK
GLA_V_W = GLA_HEADS * GLA_DV
GATE_RANK = 16
GATE_TAU = 16.0
CHUNK = 64
NAT_HEADS = 8
NAT_HD = 64
NAT_W = NAT_HEADS * NAT_HD
GRID_W = 64
MAX_KR = 8
KC = 16
FFN_HIDDEN = 2816
ALPHA = (2 * DEPTH) ** 0.25
LN_EPS = 1e-5
MASK_VALUE = -1e30

LANES = 128
LR_PAD = LANES
FB_W = GLA_HEADS * 2 * GLA_DK
QUAD = 4
QUAD_W = QUAD * NAT_HD
N_QUADS = NAT_HEADS // QUAD
WIN_KEYS = MAX_KR * GRID_W
GLA_BLOCK_CHUNKS = 8
GLA_LOOKAHEAD = 2
NAT_ROWS_PER_ITER = 4
NAT_LOOKAHEAD = 2
LOG2E = 1.4426950408889634

VMEM_LIMIT = 56 * 1024 * 1024

PROJ_OUT_W = (2 * GLA_QK_W, GLA_V_W, GLA_V_W, NAT_W, NAT_W, NAT_W, LR_PAD)
PROJ_OUT_DT = (jnp.float32, jnp.bfloat16, jnp.float32, jnp.bfloat16, jnp.bfloat16,
               jnp.bfloat16, jnp.float32)
G_SPLIT = 3


def _cparams(n_grid):
    return pltpu.CompilerParams(
        dimension_semantics=("arbitrary",) * n_grid, vmem_limit_bytes=VMEM_LIMIT)


def _inproj_kernel(x_ref, w_ref, *o_refs):
    xb = x_ref[...].astype(jnp.bfloat16)
    off = 0
    for o_ref, width in zip(o_refs, PROJ_OUT_W):
        acc = jnp.dot(xb, w_ref[:, off:off + width], preferred_element_type=jnp.float32)
        o_ref[...] = acc.astype(o_ref.dtype)
        off += width


def _inproj(x2d, w_perm, tm):
    t = x2d.shape[0]
    wtot = w_perm.shape[1]
    return pl.pallas_call(
        _inproj_kernel,
        out_shape=[jax.ShapeDtypeStruct((t, w), dt) for w, dt in zip(PROJ_OUT_W, PROJ_OUT_DT)],
        grid=(t // tm,),
        in_specs=[pl.BlockSpec((tm, D_MODEL), lambda i: (i, 0)),
                  pl.BlockSpec((D_MODEL, wtot), lambda i: (0, 0))],
        out_specs=[pl.BlockSpec((tm, w), lambda i: (i, 0)) for w in PROJ_OUT_W],
        compiler_params=_cparams(1),
        name="inproj",
    )(x2d, w_perm)


def _fb_expand(x):
    lane = lax.broadcasted_iota(jnp.int32, (x.shape[0], LANES), 1)
    low = lane < GLA_DK
    slabs = []
    for p in range(GLA_HEADS // 2):
        pair = x[:, p * LANES:(p + 1) * LANES]
        swapped = pltpu.roll(pair, GLA_DK, axis=1)
        slabs.append(jnp.where(low, pair, swapped))
        slabs.append(jnp.where(low, swapped, pair))
    return jnp.concatenate(slabs, axis=1)


def _gla_kernel(qk_ref, v_ref, r_ref, lr_ref, w2_ref, b2_ref, ng_ref, o_ref,
                qd_s, ki_s, dec_s, dst_s, stf_s, st_s, *, n_chunks):
    f32, bf16 = jnp.float32, jnp.bfloat16
    nb = GLA_BLOCK_CHUNKS
    n_blocks = n_chunks // nb
    blk = nb * CHUNK
    lane_fb = lax.broadcasted_iota(jnp.int32, (1, FB_W), 1)
    fwd_lane = (lane_fb % LANES) < GLA_DK
    fwd_lane_h = lax.broadcasted_iota(jnp.int32, (1, LANES), 1) < GLA_DK
    row = lax.broadcasted_iota(jnp.int32, (CHUNK, CHUNK), 0)
    col = lax.broadcasted_iota(jnp.int32, (CHUNK, CHUNK), 1)
    tril = row >= col
    tri_g = jnp.concatenate([tril.astype(bf16)] * G_SPLIT, axis=1)
    heads = [slice(h * LANES, (h + 1) * LANES) for h in range(GLA_HEADS)]
    chunks = [slice(j * CHUNK, (j + 1) * CHUNK) for j in range(nb)]

    st_s[...] = jnp.zeros_like(st_s)

    def pass1(b, carry):
        rows = pl.ds(pl.multiple_of(b * blk, blk), blk)
        c0 = b * nb
        z = jnp.dot(lr_ref[rows, :].astype(bf16), w2_ref[...],
                    preferred_element_type=f32) + b2_ref[...]
        g = (jnp.minimum(z, 0.0) - jnp.log(1.0 + jnp.exp(-jnp.abs(z)))) * (LOG2E / GATE_TAU)
        pieces, res = [], g
        for i in range(G_SPLIT):
            pieces.append(res.astype(bf16))
            if i + 1 < G_SPLIT:
                res = res - pieces[i].astype(f32)
        pre = [jnp.dot(tri_g, jnp.concatenate([p[cs] for p in pieces], axis=0),
                       preferred_element_type=f32) for cs in chunks]
        qq = _fb_expand(qk_ref[rows, :GLA_QK_W])
        kk = _fb_expand(qk_ref[rows, GLA_QK_W:])
        ke, dec = [], []
        for j, cs in enumerate(chunks):
            tot = pre[j][CHUNK - 1:CHUNK, :]
            bc = jnp.where(fwd_lane, pre[j], tot - pre[j] + g[cs])
            dec.append(jnp.exp2(tot))
            dec_s[c0 + j] = dec[j]
            c_rows = pl.ds(pl.multiple_of((c0 + j) * CHUNK, CHUNK), CHUNK)
            qd_s[c_rows, :] = (qq[cs] * jnp.exp2(bc)).astype(bf16)
            ki_s[c_rows, :] = (kk[cs] * jnp.exp2(-bc)).astype(bf16)
            ke.append((kk[cs] * jnp.exp2(tot - bc)).astype(bf16))
        v_blk = v_ref[rows, :]
        d_st = [[lax.dot_general(v_blk[cs, sl], ke[j][:, sl], (((0,), (0,)), ((), ())),
                                 preferred_element_type=f32)
                 for sl in heads] for j, cs in enumerate(chunks)]
        for h, sl in enumerate(heads):
            st = st_s[h]
            for j in range(nb):
                dst_s[(c0 + j) * GLA_HEADS + h] = d_st[j][h]
                stf_s[(c0 + j) * GLA_HEADS + h] = st.astype(bf16)
                st = st * dec[j][:, sl] + d_st[j][h]
            st_s[h] = st
        return carry

    lax.fori_loop(0, n_blocks, pass1, 0)

    st_s[...] = jnp.zeros_like(st_s)

    def pass2(i, carry):
        b = n_blocks - 1 - i
        rows = pl.ds(pl.multiple_of(b * blk, blk), blk)
        c0 = b * nb
        qd = qd_s[rows, :]
        ki = ki_s[rows, :]
        v_blk = v_ref[rows, :]
        st_c = [[None] * GLA_HEADS for _ in range(nb)]
        for h, sl in enumerate(heads):
            st = st_s[h]
            for j in reversed(range(nb)):
                idx = (c0 + j) * GLA_HEADS + h
                st_c[j][h] = jnp.where(fwd_lane_h, stf_s[idx], st.astype(bf16))
                st = st * dec_s[c0 + j][:, sl] + dst_s[idx]
            st_s[h] = st
        zero = jnp.zeros((CHUNK, LANES), bf16)

        def scores(j):
            out = []
            for sl in heads:
                qd_h = qd[chunks[j], sl]
                lhs = jnp.concatenate([jnp.where(fwd_lane_h, qd_h, zero),
                                       jnp.where(fwd_lane_h, zero, qd_h)], axis=0)
                out.append(lax.dot_general(lhs, ki[chunks[j], sl], (((1,), (1,)), ((), ())),
                                           preferred_element_type=f32))
            return out

        r_blk = r_ref[rows, :]
        gate = r_blk * (1.0 / (1.0 + jnp.exp(-r_blk))) * ng_ref[...]
        pending = {j: scores(j) for j in range(min(GLA_LOOKAHEAD, nb))}
        for j, cs in enumerate(chunks):
            sc = pending.pop(j)
            if j + GLA_LOOKAHEAD < nb:
                pending[j + GLA_LOOKAHEAD] = scores(j + GLA_LOOKAHEAD)
            c_rows = pl.ds(pl.multiple_of((c0 + j) * CHUNK, CHUNK), CHUNK)
            for h, sl in enumerate(heads):
                a = jnp.where(tril, sc[h][:CHUNK], sc[h][CHUNK:]).astype(bf16)
                o_h = jnp.dot(a, v_blk[cs, sl], preferred_element_type=f32)
                o_h = o_h + lax.dot_general(qd[cs, sl], st_c[j][h], (((1,), (1,)), ((), ())),
                                            preferred_element_type=f32)
                o_h = o_h * lax.rsqrt(jnp.mean(o_h * o_h, axis=-1, keepdims=True) + LN_EPS)
                o_ref[c_rows, sl] = (o_h * gate[cs, sl]).astype(o_ref.dtype)
        return carry

    lax.fori_loop(0, n_blocks, pass2, 0)


def _gla(qk, v, r, lr, w2p, b2p, ng):
    b, l, _ = qk.shape
    n_chunks = l // CHUNK
    seq = lambda w: pl.BlockSpec((None, l, w), lambda i: (i, 0, 0))
    const = lambda s: pl.BlockSpec(s, lambda i: (0,) * len(s))
    return pl.pallas_call(
        functools.partial(_gla_kernel, n_chunks=n_chunks),
        out_shape=jax.ShapeDtypeStruct((b, l, GLA_V_W), jnp.bfloat16),
        grid=(b,),
        in_specs=[seq(2 * GLA_QK_W), seq(GLA_V_W), seq(GLA_V_W), seq(LR_PAD),
                  const((LR_PAD, FB_W)), const((1, FB_W)), const((1, GLA_V_W))],
        out_specs=seq(GLA_V_W),
        scratch_shapes=[pltpu.VMEM((l, FB_W), jnp.bfloat16),
                        pltpu.VMEM((l, FB_W), jnp.bfloat16),
                        pltpu.VMEM((n_chunks, 1, FB_W), jnp.float32),
                        pltpu.VMEM((n_chunks * GLA_HEADS, GLA_DV, LANES), jnp.float32),
                        pltpu.VMEM((n_chunks * GLA_HEADS, GLA_DV, LANES), jnp.bfloat16),
                        pltpu.VMEM((GLA_HEADS, GLA_DV, LANES), jnp.float32)],
        compiler_params=_cparams(1),
        name="gla",
    )(qk, v, r, lr, w2p, b2p, ng)


def _nat_kernel(q_ref, k_ref, v_ref, bias_ref, o_ref, *, n_rows):
    f32, bf16 = jnp.float32, jnp.bfloat16
    blk_r = lax.broadcasted_iota(jnp.int32, (QUAD_W, QUAD_W), 0) // NAT_HD
    blk_c = lax.broadcasted_iota(jnp.int32, (QUAD_W, QUAD_W), 1) // NAT_HD
    blockdiag = blk_r == blk_c
    out_head = lax.broadcasted_iota(jnp.int32, (GRID_W, QUAD_W), 1) // NAT_HD

    def rows_body(i, carry):
        items = []
        for t in range(NAT_ROWS_PER_ITER):
            r = i * NAT_ROWS_PER_ITER + t
            rs = jnp.clip(r - MAX_KR // 2, 0, n_rows - MAX_KR)
            q_rows = pl.ds(pl.multiple_of(r * GRID_W, GRID_W), GRID_W)
            w_rows = pl.ds(pl.multiple_of(rs * GRID_W, GRID_W), WIN_KEYS)
            for quad in range(N_QUADS):
                items.append((q_rows, w_rows, r - rs, quad,
                              slice(quad * QUAD_W, (quad + 1) * QUAD_W)))
        def score(item):
            q_rows, w_rows, e, quad, sl = item
            q_r = q_ref[q_rows, sl]
            q_bd = jnp.where(blockdiag, jnp.concatenate([q_r] * QUAD, axis=0),
                             jnp.zeros((QUAD_W, QUAD_W), bf16))
            return lax.dot_general(q_bd, k_ref[w_rows, sl], (((1,), (1,)), ((), ())),
                                   preferred_element_type=f32)

        def attend(item, s):
            q_rows, w_rows, e, quad, sl = item
            s = s + bias_ref[e, quad]
            p = jnp.exp2(s - jnp.max(s, axis=-1, keepdims=True))
            denom = jnp.sum(p, axis=-1, keepdims=True)
            return p.astype(bf16), denom

        def finish(item, p, denom):
            q_rows, w_rows, e, quad, sl = item
            o = jnp.dot(p, v_ref[w_rows, sl], preferred_element_type=f32)
            o = o * (1.0 / denom)
            acc = jnp.zeros((GRID_W, QUAD_W), f32)
            for h in range(QUAD):
                acc = acc + jnp.where(out_head == h, o[h * GRID_W:(h + 1) * GRID_W, :], 0.0)
            o_ref[q_rows, sl] = acc.astype(o_ref.dtype)

        pending = {k: score(items[k]) for k in range(min(NAT_LOOKAHEAD, len(items)))}
        for k, item in enumerate(items):
            p, denom = attend(item, pending.pop(k))
            if k + NAT_LOOKAHEAD < len(items):
                pending[k + NAT_LOOKAHEAD] = score(items[k + NAT_LOOKAHEAD])
            finish(item, p, denom)
        return carry

    lax.fori_loop(0, n_rows // NAT_ROWS_PER_ITER, rows_body, 0)


def _nat(q, k, v, bias):
    b, l, _ = q.shape
    n_rows = l // GRID_W
    seq = pl.BlockSpec((None, l, NAT_W), lambda i: (i, 0, 0))
    return pl.pallas_call(
        functools.partial(_nat_kernel, n_rows=n_rows),
        out_shape=jax.ShapeDtypeStruct((b, l, NAT_W), jnp.bfloat16),
        grid=(b,),
        in_specs=[seq, seq, seq,
                  pl.BlockSpec(bias.shape, lambda i: (0, 0, 0, 0))],
        out_specs=seq,
        compiler_params=_cparams(1),
        name="nat",
    )(q, k, v, bias)


def _nat_bias_table(rpb):
    c = np.arange(GRID_W)[:, None]
    m = np.arange(GRID_W)[None, :]
    col_start = np.clip(c - KC // 2, 0, GRID_W - KC)
    valid = (m >= col_start) & (m < col_start + KC)
    co = np.clip(m - c + KC - 1, 0, 2 * KC - 2)
    onehot = (co[:, :, None] == np.arange(2 * KC - 1)).astype(np.float32)
    toep = jnp.einsum("hrt,cmt->hrcm", rpb * LOG2E, jnp.asarray(onehot),
                      precision=lax.Precision.HIGHEST)
    toep = jnp.where(jnp.asarray(valid)[None, None], toep, MASK_VALUE)
    t = jnp.stack([toep[:, MAX_KR - 1 - e:2 * MAX_KR - 1 - e] for e in range(MAX_KR)])
    t = jnp.transpose(t, (0, 1, 3, 2, 4))
    return t.reshape(MAX_KR, N_QUADS, QUAD * GRID_W, MAX_KR * GRID_W).astype(jnp.float32)


def _layer_norm(y, g, b):
    mu = jnp.mean(y, axis=-1, keepdims=True)
    yc = y - mu
    var = jnp.mean(yc * yc, axis=-1, keepdims=True)
    return yc * lax.rsqrt(var + LN_EPS) * g + b


def _mix_ffn_kernel(oa_ref, on_ref, x_ref, wa_ref, wn_ref, g1_ref, b1_ref,
                    w1_ref, w2_ref, g2_ref, b2_ref, o_ref):
    f32, bf16 = jnp.float32, jnp.bfloat16
    mix = jnp.dot(oa_ref[...], wa_ref[...], preferred_element_type=f32)
    mix = mix + jnp.dot(on_ref[...], wn_ref[...], preferred_element_type=f32)
    x1 = _layer_norm(ALPHA * x_ref[...] + mix, g1_ref[...], b1_ref[...])
    h = jnp.dot(x1.astype(bf16), w1_ref[...], preferred_element_type=f32)
    gate = h[:, :FFN_HIDDEN]
    act = gate * (1.0 / (1.0 + jnp.exp(-gate))) * h[:, FFN_HIDDEN:]
    y = jnp.dot(act.astype(bf16), w2_ref[...], preferred_element_type=f32)
    o_ref[...] = _layer_norm(ALPHA * x1 + y, g2_ref[...], b2_ref[...])


def _mix_ffn(oa, on, x2d, p, tm):
    t = x2d.shape[0]
    tile = lambda w: pl.BlockSpec((tm, w), lambda i: (i, 0))
    const = lambda s: pl.BlockSpec(s, lambda i: (0, 0), pipeline_mode=pl.Buffered(1))
    vec = const((1, D_MODEL))
    return pl.pallas_call(
        _mix_ffn_kernel,
        out_shape=jax.ShapeDtypeStruct((t, D_MODEL), jnp.float32),
        grid=(t // tm,),
        in_specs=[tile(GLA_V_W), tile(NAT_W), tile(D_MODEL),
                  const((GLA_V_W, D_MODEL)), const((NAT_W, D_MODEL)), vec, vec,
                  const((D_MODEL, 2 * FFN_HIDDEN)), const((FFN_HIDDEN, D_MODEL)), vec, vec],
        out_specs=tile(D_MODEL),
        compiler_params=_cparams(1),
        name="mix_ffn",
    )(oa, on, x2d, p["wa"], p["wn"], p["ln1_g"], p["ln1_b"],
      p["w1"], p["w2"], p["ln2_g"], p["ln2_b"])


def _prep_layer(w_in, gate_w2, gate_b, norm_g, rpb, w_out, ln1_g, ln1_b,
                w_ffn_in, w_ffn_out, ln2_g, ln2_b):
    f32, bf16 = jnp.float32, jnp.bfloat16
    sizes = (GLA_QK_W, GLA_QK_W, GLA_V_W, GLA_V_W, 2 * GATE_RANK, NAT_W, NAT_W, NAT_W)
    pts = np.cumsum((0,) + sizes)
    q_a, k_a, v_a, r_a, lr, q_b, k_b, v_b = [w_in[:, pts[i]:pts[i + 1]] for i in range(8)]
    lr = jnp.pad(lr, ((0, 0), (0, LR_PAD - 2 * GATE_RANK)))
    w_perm = jnp.concatenate(
        [q_a * GLA_DK ** -0.5, k_a, v_a, r_a, q_b * (NAT_HD ** -0.5 * LOG2E), k_b, v_b, lr],
        axis=1)
    w2p = jnp.zeros((LR_PAD, GLA_HEADS, 2, GLA_DK), f32)
    w2p = w2p.at[:GATE_RANK, :, 0, :].set(gate_w2[0].reshape(GATE_RANK, GLA_HEADS, GLA_DK))
    w2p = w2p.at[GATE_RANK:2 * GATE_RANK, :, 1, :].set(
        gate_w2[1].reshape(GATE_RANK, GLA_HEADS, GLA_DK))
    b2p = jnp.stack([gate_b[0].reshape(GLA_HEADS, GLA_DK),
                     gate_b[1].reshape(GLA_HEADS, GLA_DK)], axis=1)
    return dict(
        w_perm=w_perm.astype(bf16),
        w2p=w2p.reshape(LR_PAD, FB_W).astype(bf16),
        b2p=b2p.reshape(1, FB_W).astype(f32),
        ng=norm_g.reshape(1, GLA_V_W).astype(f32),
        bias=_nat_bias_table(rpb.astype(f32)),
        wa=w_out[:GLA_V_W].astype(bf16), wn=w_out[GLA_V_W:].astype(bf16),
        ln1_g=ln1_g.reshape(1, D_MODEL), ln1_b=ln1_b.reshape(1, D_MODEL),
        w1=w_ffn_in.astype(bf16), w2=w_ffn_out.astype(bf16),
        ln2_g=ln2_g.reshape(1, D_MODEL), ln2_b=ln2_b.reshape(1, D_MODEL),
    )


def _row_tile(t):
    return 512 if t % 512 == 0 else t


def _layer(x2d, b, l, p):
    tm = _row_tile(x2d.shape[0])
    qk, v_a, r_a, q_b, k_b, v_b, lr = _inproj(x2d, p["w_perm"], tm)
    seq = lambda a: a.reshape(b, l, a.shape[-1])
    o_a = _gla(seq(qk), seq(v_a), seq(r_a), seq(lr), p["w2p"], p["b2p"], p["ng"])
    o_n = _nat(seq(q_b), seq(k_b), seq(v_b), p["bias"])
    return _mix_ffn(o_a.reshape(b * l, GLA_V_W), o_n.reshape(b * l, NAT_W), x2d, p, tm)


def _trunk(x, layers):
    b, l, d = x.shape
    x2d = x.reshape(b * l, d)
    for p in layers:
        x2d = _layer(x2d, b, l, p)
    return x2d.reshape(b, l, d)


def kernel(x_prompt, x_sample, w_in, gla_gate_w2, gla_gate_b, gla_norm_g, nat_rpb, w_out,
           ln1_g, ln1_b, w_ffn_in, w_ffn_out, ln2_g, ln2_b):
    layers = [_prep_layer(w_in[i], gla_gate_w2[i], gla_gate_b[i], gla_norm_g[i], nat_rpb[i],
                          w_out[i], ln1_g[i], ln1_b[i], w_ffn_in[i], w_ffn_out[i],
                          ln2_g[i], ln2_b[i]) for i in range(DEPTH)]
    return (_trunk(x_prompt, layers), _trunk(x_sample, layers))
```

```python
import functools

import numpy as np
import jax
import jax.numpy as jnp
from jax import lax
from jax.experimental import pallas as pl
from jax.experimental.pallas import tpu as pltpu

D_MODEL = 1024
DEPTH = 2
GLA_HEADS = 4
GLA_DK = 64
GLA_DV = 128
GLA_QK_W = GLA_HEADS * GLA_DK
GLA_V_W = GLA_HEADS * GLA_DV
GATE_RANK = 16
GATE_TAU = 16.0
CHUNK = 64
NAT_HEADS = 8
NAT_HD = 64
NAT_W = NAT_HEADS * NAT_HD
GRID_W = 64
MAX_KR = 8
KC = 16
FFN_HIDDEN = 2816
ALPHA = (2 * DEPTH) ** 0.25
LN_EPS = 1e-5
MASK_VALUE = -1e30

LANES = 128
LR_PAD = LANES
FB_W = GLA_HEADS * 2 * GLA_DK
QUAD = 4
QUAD_W = QUAD * NAT_HD
N_QUADS = NAT_HEADS // QUAD
WIN_KEYS = MAX_KR * GRID_W
GLA_BLOCK_CHUNKS = 8
GLA_LOOKAHEAD = 2
NAT_ROWS_PER_ITER = 8
NAT_LOOKAHEAD = 2
FFN_ROW_GROUPS = 2
LOG2E = 1.4426950408889634

VMEM_LIMIT = 56 * 1024 * 1024

QKL_W = 2 * GLA_QK_W + LR_PAD
PROJ_OUT_W = (QKL_W, GLA_V_W, GLA_V_W, NAT_W, NAT_W, NAT_W)
PROJ_OUT_DT = (jnp.float32, jnp.bfloat16, jnp.float32, jnp.bfloat16, jnp.bfloat16, jnp.bfloat16)
G_SPLIT = 3


def _cparams(n_grid):
    return pltpu.CompilerParams(
        dimension_semantics=("arbitrary",) * n_grid, vmem_limit_bytes=VMEM_LIMIT)


def _inproj_kernel(x_ref, w_ref, *o_refs):
    xb = x_ref[...].astype(jnp.bfloat16)
    off = 0
    for o_ref, width in zip(o_refs, PROJ_OUT_W):
        acc = jnp.dot(xb, w_ref[:, off:off + width], preferred_element_type=jnp.float32)
        o_ref[...] = acc.astype(o_ref.dtype)
        off += width


def _inproj(x2d, w_perm, tm):
    t = x2d.shape[0]
    wtot = w_perm.shape[1]
    return pl.pallas_call(
        _inproj_kernel,
        out_shape=[jax.ShapeDtypeStruct((t, w), dt) for w, dt in zip(PROJ_OUT_W, PROJ_OUT_DT)],
        grid=(t // tm,),
        in_specs=[pl.BlockSpec((tm, D_MODEL), lambda i: (i, 0)),
                  pl.BlockSpec((D_MODEL, wtot), lambda i: (0, 0))],
        out_specs=[pl.BlockSpec((tm, w), lambda i: (i, 0)) for w in PROJ_OUT_W],
        compiler_params=_cparams(1),
        name="inproj",
    )(x2d, w_perm)


def _fb_expand(x):
    lane = lax.broadcasted_iota(jnp.int32, (x.shape[0], LANES), 1)
    low = lane < GLA_DK
    slabs = []
    for p in range(GLA_HEADS // 2):
        pair = x[:, p * LANES:(p + 1) * LANES]
        swapped = pltpu.roll(pair, GLA_DK, axis=1)
        slabs.append(jnp.where(low, pair, swapped))
        slabs.append(jnp.where(low, swapped, pair))
    return jnp.concatenate(slabs, axis=1)


def _gla_kernel(qkl_ref, v_ref, r_ref, w2_ref, b2_ref, ng_ref, o_ref,
                qd_s, ki_s, dec_s, dst_s, stf_s, st_s, *, n_chunks):
    f32, bf16 = jnp.float32, jnp.bfloat16
    nb = GLA_BLOCK_CHUNKS
    n_blocks = n_chunks // nb
    blk = nb * CHUNK
    lane_fb = lax.broadcasted_iota(jnp.int32, (1, FB_W), 1)
    fwd_lane = (lane_fb % LANES) < GLA_DK
    fwd_lane_h = lax.broadcasted_iota(jnp.int32, (1, LANES), 1) < GLA_DK
    row = lax.broadcasted_iota(jnp.int32, (CHUNK, CHUNK), 0)
    col = lax.broadcasted_iota(jnp.int32, (CHUNK, CHUNK), 1)
    tril = row >= col
    tri_g = jnp.concatenate([tril.astype(bf16)] * G_SPLIT, axis=1)
    heads = [slice(h * LANES, (h + 1) * LANES) for h in range(GLA_HEADS)]
    chunks = [slice(j * CHUNK, (j + 1) * CHUNK) for j in range(nb)]

    st_s[...] = jnp.zeros_like(st_s)

    def pass1(b, carry):
        rows = pl.ds(pl.multiple_of(b * blk, blk), blk)
        c0 = b * nb
        z = jnp.dot(qkl_ref[rows, 2 * GLA_QK_W:].astype(bf16), w2_ref[...],
                    preferred_element_type=f32) + b2_ref[...]
        g = (jnp.minimum(z, 0.0) - jnp.log(1.0 + jnp.exp(-jnp.abs(z)))) * (LOG2E / GATE_TAU)
        pieces, res = [], g
        for i in range(G_SPLIT):
            pieces.append(res.astype(bf16))
            if i + 1 < G_SPLIT:
                res = res - pieces[i].astype(f32)
        pre = [jnp.dot(tri_g, jnp.concatenate([p[cs] for p in pieces], axis=0),
                       preferred_element_type=f32) for cs in chunks]
        qq = _fb_expand(qkl_ref[rows, :GLA_QK_W])
        kk = _fb_expand(qkl_ref[rows, GLA_QK_W:2 * GLA_QK_W])
        ke, dec = [], []
        for j, cs in enumerate(chunks):
            tot = pre[j][CHUNK - 1:CHUNK, :]
            bc = jnp.where(fwd_lane, pre[j], tot - pre[j] + g[cs])
            dec.append(jnp.exp2(tot))
            dec_s[c0 + j] = dec[j]
            c_rows = pl.ds(pl.multiple_of((c0 + j) * CHUNK, CHUNK), CHUNK)
            qd_s[c_rows, :] = (qq[cs] * jnp.exp2(bc)).astype(bf16)
            ki_s[c_rows, :] = (kk[cs] * jnp.exp2(-bc)).astype(bf16)
            ke.append((kk[cs] * jnp.exp2(tot - bc)).astype(bf16))
        v_blk = v_ref[rows, :]
        d_st = [[lax.dot_general(v_blk[cs, sl], ke[j][:, sl], (((0,), (0,)), ((), ())),
                                 preferred_element_type=f32)
                 for sl in heads] for j, cs in enumerate(chunks)]
        for h, sl in enumerate(heads):
            st = st_s[h]
            for j in range(nb):
                dst_s[(c0 + j) * GLA_HEADS + h] = d_st[j][h]
                stf_s[(c0 + j) * GLA_HEADS + h] = st.astype(bf16)
                st = st * dec[j][:, sl] + d_st[j][h]
            st_s[h] = st
        return carry

    lax.fori_loop(0, n_blocks, pass1, 0)

    st_s[...] = jnp.zeros_like(st_s)

    def pass2(i, carry):
        b = n_blocks - 1 - i
        rows = pl.ds(pl.multiple_of(b * blk, blk), blk)
        c0 = b * nb
        qd = qd_s[rows, :]
        ki = ki_s[rows, :]
        v_blk = v_ref[rows, :]
        st_c = [[None] * GLA_HEADS for _ in range(nb)]
        for h, sl in enumerate(heads):
            st = st_s[h]
            for j in reversed(range(nb)):
                idx = (c0 + j) * GLA_HEADS + h
                st_c[j][h] = jnp.where(fwd_lane_h, stf_s[idx], st.astype(bf16))
                st = st * dec_s[c0 + j][:, sl] + dst_s[idx]
            st_s[h] = st
        zero = jnp.zeros((CHUNK, LANES), bf16)

        def scores(j):
            out = []
            for sl in heads:
                qd_h = qd[chunks[j], sl]
                lhs = jnp.concatenate([jnp.where(fwd_lane_h, qd_h, zero),
                                       jnp.where(fwd_lane_h, zero, qd_h)], axis=0)
                out.append(lax.dot_general(lhs, ki[chunks[j], sl], (((1,), (1,)), ((), ())),
                                           preferred_element_type=f32))
            return out

        r_blk = r_ref[rows, :]
        gate = r_blk * (1.0 / (1.0 + jnp.exp(-r_blk))) * ng_ref[...]
        pending = {j: scores(j) for j in range(min(GLA_LOOKAHEAD, nb))}
        for j, cs in enumerate(chunks):
            sc = pending.pop(j)
            if j + GLA_LOOKAHEAD < nb:
                pending[j + GLA_LOOKAHEAD] = scores(j + GLA_LOOKAHEAD)
            c_rows = pl.ds(pl.multiple_of((c0 + j) * CHUNK, CHUNK), CHUNK)
            for h, sl in enumerate(heads):
                a = jnp.where(tril, sc[h][:CHUNK], sc[h][CHUNK:]).astype(bf16)
                o_h = jnp.dot(a, v_blk[cs, sl], preferred_element_type=f32)
                o_h = o_h + lax.dot_general(qd[cs, sl], st_c[j][h], (((1,), (1,)), ((), ())),
                                            preferred_element_type=f32)
                o_h = o_h * lax.rsqrt(jnp.mean(o_h * o_h, axis=-1, keepdims=True) + LN_EPS)
                o_ref[c_rows, sl] = (o_h * gate[cs, sl]).astype(o_ref.dtype)
        return carry

    lax.fori_loop(0, n_blocks, pass2, 0)


def _gla(qkl, v, r, w2p, b2p, ng):
    b, l, _ = qkl.shape
    n_chunks = l // CHUNK
    seq = lambda w: pl.BlockSpec((None, l, w), lambda i: (i, 0, 0))
    const = lambda s: pl.BlockSpec(s, lambda i: (0,) * len(s))
    return pl.pallas_call(
        functools.partial(_gla_kernel, n_chunks=n_chunks),
        out_shape=jax.ShapeDtypeStruct((b, l, GLA_V_W), jnp.bfloat16),
        grid=(b,),
        in_specs=[seq(QKL_W), seq(GLA_V_W), seq(GLA_V_W),
                  const((LR_PAD, FB_W)), const((1, FB_W)), const((1, GLA_V_W))],
        out_specs=seq(GLA_V_W),
        scratch_shapes=[pltpu.VMEM((l, FB_W), jnp.bfloat16),
                        pltpu.VMEM((l, FB_W), jnp.bfloat16),
                        pltpu.VMEM((n_chunks, 1, FB_W), jnp.float32),
                        pltpu.VMEM((n_chunks * GLA_HEADS, GLA_DV, LANES), jnp.float32),
                        pltpu.VMEM((n_chunks * GLA_HEADS, GLA_DV, LANES), jnp.bfloat16),
                        pltpu.VMEM((GLA_HEADS, GLA_DV, LANES), jnp.float32)],
        compiler_params=_cparams(1),
        name="gla",
    )(qkl, v, r, w2p, b2p, ng)


def _nat_kernel(q_ref, k_ref, v_ref, bias_ref, o_ref, *, n_rows):
    f32, bf16 = jnp.float32, jnp.bfloat16
    blk_r = lax.broadcasted_iota(jnp.int32, (QUAD_W, QUAD_W), 0) // NAT_HD
    blk_c = lax.broadcasted_iota(jnp.int32, (QUAD_W, QUAD_W), 1) // NAT_HD
    blockdiag = blk_r == blk_c
    out_head = lax.broadcasted_iota(jnp.int32, (GRID_W, QUAD_W), 1) // NAT_HD

    def rows_body(i, carry):
        items = []
        for t in range(NAT_ROWS_PER_ITER):
            r = i * NAT_ROWS_PER_ITER + t
            rs = jnp.clip(r - MAX_KR // 2, 0, n_rows - MAX_KR)
            q_rows = pl.ds(pl.multiple_of(r * GRID_W, GRID_W), GRID_W)
            w_rows = pl.ds(pl.multiple_of(rs * GRID_W, GRID_W), WIN_KEYS)
            for quad in range(N_QUADS):
                items.append((q_rows, w_rows, r - rs, quad,
                              slice(quad * QUAD_W, (quad + 1) * QUAD_W)))
        def score(item):
            q_rows, w_rows, e, quad, sl = item
            q_r = q_ref[q_rows, sl]
            q_bd = jnp.where(blockdiag, jnp.concatenate([q_r] * QUAD, axis=0),
                             jnp.zeros((QUAD_W, QUAD_W), bf16))
            return lax.dot_general(q_bd, k_ref[w_rows, sl], (((1,), (1,)), ((), ())),
                                   preferred_element_type=f32)

        def attend(item, s):
            q_rows, w_rows, e, quad, sl = item
            s = s + bias_ref[e, quad]
            p = jnp.exp2(s - jnp.max(s, axis=-1, keepdims=True))
            denom = jnp.sum(p, axis=-1, keepdims=True)
            return p.astype(bf16), denom

        def finish(item, p, denom):
            q_rows, w_rows, e, quad, sl = item
            o = jnp.dot(p, v_ref[w_rows, sl], preferred_element_type=f32)
            o = o * (1.0 / denom)
            acc = jnp.zeros((GRID_W, QUAD_W), f32)
            for h in range(QUAD):
                acc = acc + jnp.where(out_head == h, o[h * GRID_W:(h + 1) * GRID_W, :], 0.0)
            o_ref[q_rows, sl] = acc.astype(o_ref.dtype)

        pending = {k: score(items[k]) for k in range(min(NAT_LOOKAHEAD, len(items)))}
        for k, item in enumerate(items):
            p, denom = attend(item, pending.pop(k))
            if k + NAT_LOOKAHEAD < len(items):
                pending[k + NAT_LOOKAHEAD] = score(items[k + NAT_LOOKAHEAD])
            finish(item, p, denom)
        return carry

    lax.fori_loop(0, n_rows // NAT_ROWS_PER_ITER, rows_body, 0)


def _nat(q, k, v, bias):
    b, l, _ = q.shape
    n_rows = l // GRID_W
    seq = pl.BlockSpec((None, l, NAT_W), lambda i: (i, 0, 0))
    return pl.pallas_call(
        functools.partial(_nat_kernel, n_rows=n_rows),
        out_shape=jax.ShapeDtypeStruct((b, l, NAT_W), jnp.bfloat16),
        grid=(b,),
        in_specs=[seq, seq, seq,
                  pl.BlockSpec(bias.shape, lambda i: (0, 0, 0, 0))],
        out_specs=seq,
        compiler_params=_cparams(1),
        name="nat",
    )(q, k, v, bias)


def _nat_bias_table(rpb):
    c = np.arange(GRID_W)[:, None]
    m = np.arange(GRID_W)[None, :]
    col_start = np.clip(c - KC // 2, 0, GRID_W - KC)
    valid = (m >= col_start) & (m < col_start + KC)
    co = np.clip(m - c + KC - 1, 0, 2 * KC - 2)
    onehot = (co[:, :, None] == np.arange(2 * KC - 1)).astype(np.float32)
    toep = jnp.einsum("hrt,cmt->hrcm", rpb * LOG2E, jnp.asarray(onehot),
                      precision=lax.Precision.HIGHEST)
    toep = jnp.where(jnp.asarray(valid)[None, None], toep, MASK_VALUE)
    t = jnp.stack([toep[:, MAX_KR - 1 - e:2 * MAX_KR - 1 - e] for e in range(MAX_KR)])
    t = jnp.transpose(t, (0, 1, 3, 2, 4))
    return t.reshape(MAX_KR, N_QUADS, QUAD * GRID_W, MAX_KR * GRID_W).astype(jnp.float32)


def _layer_norm(y, g, b):
    mu = jnp.mean(y, axis=-1, keepdims=True)
    yc = y - mu
    var = jnp.mean(yc * yc, axis=-1, keepdims=True)
    return yc * lax.rsqrt(var + LN_EPS) * g + b


def _mix_ffn_kernel(oa_ref, on_ref, x_ref, wa_ref, wn_ref, g1_ref, b1_ref,
                    w1_ref, w2_ref, g2_ref, b2_ref, o_ref):
    f32, bf16 = jnp.float32, jnp.bfloat16
    rows_per_group = x_ref.shape[0] // FFN_ROW_GROUPS
    groups = [slice(i * rows_per_group, (i + 1) * rows_per_group) for i in range(FFN_ROW_GROUPS)]
    mix = [jnp.dot(oa_ref[rows, :], wa_ref[...], preferred_element_type=f32)
           + jnp.dot(on_ref[rows, :], wn_ref[...], preferred_element_type=f32) for rows in groups]
    x1 = [_layer_norm(ALPHA * x_ref[rows, :] + m, g1_ref[...], b1_ref[...])
          for rows, m in zip(groups, mix)]
    h = [jnp.dot(x.astype(bf16), w1_ref[...], preferred_element_type=f32) for x in x1]
    act = []
    for hh in h:
        gate = hh[:, :FFN_HIDDEN]
        act.append((gate * (1.0 / (1.0 + jnp.exp(-gate))) * hh[:, FFN_HIDDEN:]).astype(bf16))
    y = [jnp.dot(a, w2_ref[...], preferred_element_type=f32) for a in act]
    for rows, x, yy in zip(groups, x1, y):
        o_ref[rows, :] = _layer_norm(ALPHA * x + yy, g2_ref[...], b2_ref[...])


def _mix_ffn(oa, on, x2d, p, tm):
    t = x2d.shape[0]
    tile = lambda w: pl.BlockSpec((tm, w), lambda i: (i, 0))
    const = lambda s: pl.BlockSpec(s, lambda i: (0, 0), pipeline_mode=pl.Buffered(1))
    vec = const((1, D_MODEL))
    return pl.pallas_call(
        _mix_ffn_kernel,
        out_shape=jax.ShapeDtypeStruct((t, D_MODEL), jnp.float32),
        grid=(t // tm,),
        in_specs=[tile(GLA_V_W), tile(NAT_W), tile(D_MODEL),
                  const((GLA_V_W, D_MODEL)), const((NAT_W, D_MODEL)), vec, vec,
                  const((D_MODEL, 2 * FFN_HIDDEN)), const((FFN_HIDDEN, D_MODEL)), vec, vec],
        out_specs=tile(D_MODEL),
        compiler_params=_cparams(1),
        name="mix_ffn",
    )(oa, on, x2d, p["wa"], p["wn"], p["ln1_g"], p["ln1_b"],
      p["w1"], p["w2"], p["ln2_g"], p["ln2_b"])


def _prep_layer(w_in, gate_w2, gate_b, norm_g, rpb, w_out, ln1_g, ln1_b,
                w_ffn_in, w_ffn_out, ln2_g, ln2_b):
    f32, bf16 = jnp.float32, jnp.bfloat16
    sizes = (GLA_QK_W, GLA_QK_W, GLA_V_W, GLA_V_W, 2 * GATE_RANK, NAT_W, NAT_W, NAT_W)
    pts = np.cumsum((0,) + sizes)
    q_a, k_a, v_a, r_a, lr, q_b, k_b, v_b = [w_in[:, pts[i]:pts[i + 1]] for i in range(8)]
    lr = jnp.pad(lr, ((0, 0), (0, LR_PAD - 2 * GATE_RANK)))
    w_perm = jnp.concatenate(
        [q_a * GLA_DK ** -0.5, k_a, lr, v_a, r_a, q_b * (NAT_HD ** -0.5 * LOG2E), k_b, v_b],
        axis=1)
    w2p = jnp.zeros((LR_PAD, GLA_HEADS, 2, GLA_DK), f32)
    w2p = w2p.at[:GATE_RANK, :, 0, :].set(gate_w2[0].reshape(GATE_RANK, GLA_HEADS, GLA_DK))
    w2p = w2p.at[GATE_RANK:2 * GATE_RANK, :, 1, :].set(
        gate_w2[1].reshape(GATE_RANK, GLA_HEADS, GLA_DK))
    b2p = jnp.stack([gate_b[0].reshape(GLA_HEADS, GLA_DK),
                     gate_b[1].reshape(GLA_HEADS, GLA_DK)], axis=1)
    return dict(
        w_perm=w_perm.astype(bf16),
        w2p=w2p.reshape(LR_PAD, FB_W).astype(bf16),
        b2p=b2p.reshape(1, FB_W).astype(f32),
        ng=norm_g.reshape(1, GLA_V_W).astype(f32),
        bias=_nat_bias_table(rpb.astype(f32)),
        wa=w_out[:GLA_V_W].astype(bf16), wn=w_out[GLA_V_W:].astype(bf16),
        ln1_g=ln1_g.reshape(1, D_MODEL), ln1_b=ln1_b.reshape(1, D_MODEL),
        w1=w_ffn_in.astype(bf16), w2=w_ffn_out.astype(bf16),
        ln2_g=ln2_g.reshape(1, D_MODEL), ln2_b=ln2_b.reshape(1, D_MODEL),
    )


def _row_tile(t):
    return 512 if t % 512 == 0 else t


def _layer(x2d, b, l, p):
    tm = _row_tile(x2d.shape[0])
    qkl, v_a, r_a, q_b, k_b, v_b = _inproj(x2d, p["w_perm"], tm)
    seq = lambda a: a.reshape(b, l, a.shape[-1])
    o_a = _gla(seq(qkl), seq(v_a), seq(r_a), p["w2p"], p["b2p"], p["ng"])
    o_n = _nat(seq(q_b), seq(k_b), seq(v_b), p["bias"])
    return _mix_ffn(o_a.reshape(b * l, GLA_V_W), o_n.reshape(b * l, NAT_W), x2d, p, tm)


def _trunk(x, layers):
    b, l, d = x.shape
    x2d = x.reshape(b * l, d)
    for p in layers:
        x2d = _layer(x2d, b, l, p)
    return x2d.reshape(b, l, d)


def kernel(x_prompt, x_sample, w_in, gla_gate_w2, gla_gate_b, gla_norm_g, nat_rpb, w_out,
           ln1_g, ln1_b, w_ffn_in, w_ffn_out, ln2_g, ln2_b):
    layers = [_prep_layer(w_in[i], gla_gate_w2[i], gla_gate_b[i], gla_norm_g[i], nat_rpb[i],
                          w_out[i], ln1_g[i], ln1_b[i], w_ffn_in[i], w_ffn_out[i],
                          ln2_g[i], ln2_b[i]) for i in range(DEPTH)]
    return (_trunk(x_prompt, layers), _trunk(x_sample, layers))
```

```python
import functools

import numpy as np
import jax
import jax.numpy as jnp
from jax import lax
from jax.experimental import pallas as pl
from jax.experimental.pallas import tpu as pltpu

D_MODEL = 1024
DEPTH = 2
GLA_HEADS = 4
GLA_DK = 64
GLA_DV = 128
GLA_QK_W = GLA_HEADS * GLA_DK
GLA_V_W = GLA_HEADS * GLA_DV
GATE_RANK = 16
GATE_TAU = 16.0
CHUNK = 64
NAT_HEADS = 8
NAT_HD = 64
NAT_W = NAT_HEADS * NAT_HD
GRID_W = 64
MAX_KR = 8
KC = 16
FFN_HIDDEN = 2816
ALPHA = (2 * DEPTH) ** 0.25
LN_EPS = 1e-5
MASK_VALUE = -1e30

LANES = 128
LR_PAD = LANES
FB_W = GLA_HEADS * 2 * GLA_DK
QUAD = 4
QUAD_W = QUAD * NAT_HD
N_QUADS = NAT_HEADS // QUAD
WIN_KEYS = MAX_KR * GRID_W
GLA_BLOCK_CHUNKS = 8
GLA_LOOKAHEAD = 2
NAT_ROWS_PER_ITER = 16
NAT_LOOKAHEAD = 2
PROJ_ROW_TILE = 1024
FFN_ROW_TILE = 512
FFN_ROW_GROUPS = 2
LOG2E = 1.4426950408889634

VMEM_LIMIT = 56 * 1024 * 1024

QKL_W = 2 * GLA_QK_W + LR_PAD
PROJ_OUT_W = (QKL_W, GLA_V_W, GLA_V_W, NAT_W, NAT_W, NAT_W)
PROJ_OUT_DT = (jnp.float32, jnp.bfloat16, jnp.float32, jnp.bfloat16, jnp.bfloat16, jnp.bfloat16)
G_SPLIT = 2


def _cparams(n_grid):
    return pltpu.CompilerParams(
        dimension_semantics=("arbitrary",) * n_grid, vmem_limit_bytes=VMEM_LIMIT)


def _inproj_kernel(x_ref, w_ref, *o_refs):
    xb = x_ref[...].astype(jnp.bfloat16)
    off = 0
    for o_ref, width in zip(o_refs, PROJ_OUT_W):
        acc = jnp.dot(xb, w_ref[:, off:off + width], preferred_element_type=jnp.float32)
        o_ref[...] = acc.astype(o_ref.dtype)
        off += width


def _inproj(x2d, w_perm, tm):
    t = x2d.shape[0]
    wtot = w_perm.shape[1]
    return pl.pallas_call(
        _inproj_kernel,
        out_shape=[jax.ShapeDtypeStruct((t, w), dt) for w, dt in zip(PROJ_OUT_W, PROJ_OUT_DT)],
        grid=(t // tm,),
        in_specs=[pl.BlockSpec((tm, D_MODEL), lambda i: (i, 0)),
                  pl.BlockSpec((D_MODEL, wtot), lambda i: (0, 0))],
        out_specs=[pl.BlockSpec((tm, w), lambda i: (i, 0)) for w in PROJ_OUT_W],
        compiler_params=_cparams(1),
        name="inproj",
    )(x2d, w_perm)


def _fb_expand(x):
    lane = lax.broadcasted_iota(jnp.int32, (x.shape[0], LANES), 1)
    low = lane < GLA_DK
    slabs = []
    for p in range(GLA_HEADS // 2):
        pair = x[:, p * LANES:(p + 1) * LANES]
        swapped = pltpu.roll(pair, GLA_DK, axis=1)
        slabs.append(jnp.where(low, pair, swapped))
        slabs.append(jnp.where(low, swapped, pair))
    return jnp.concatenate(slabs, axis=1)


def _gla_kernel(qkl_ref, v_ref, r_ref, w2_ref, b2_ref, ng_ref, o_ref,
                qd_s, ki_s, dec_s, dst_s, stf_s, st_s, *, n_chunks):
    f32, bf16 = jnp.float32, jnp.bfloat16
    nb = GLA_BLOCK_CHUNKS
    n_blocks = n_chunks // nb
    blk = nb * CHUNK
    lane_fb = lax.broadcasted_iota(jnp.int32, (1, FB_W), 1)
    fwd_lane = (lane_fb % LANES) < GLA_DK
    fwd_lane_h = lax.broadcasted_iota(jnp.int32, (1, LANES), 1) < GLA_DK
    row = lax.broadcasted_iota(jnp.int32, (CHUNK, CHUNK), 0)
    col = lax.broadcasted_iota(jnp.int32, (CHUNK, CHUNK), 1)
    tril = row >= col
    tri_g = jnp.concatenate([tril.astype(bf16)] * G_SPLIT, axis=1)
    heads = [slice(h * LANES, (h + 1) * LANES) for h in range(GLA_HEADS)]
    chunks = [slice(j * CHUNK, (j + 1) * CHUNK) for j in range(nb)]

    st_s[...] = jnp.zeros_like(st_s)

    def pass1(b, carry):
        rows = pl.ds(pl.multiple_of(b * blk, blk), blk)
        c0 = b * nb
        z = jnp.dot(qkl_ref[rows, 2 * GLA_QK_W:].astype(bf16), w2_ref[...],
                    preferred_element_type=f32) + b2_ref[...]
        g = (jnp.minimum(z, 0.0) - jnp.log(1.0 + jnp.exp(-jnp.abs(z)))) * (LOG2E / GATE_TAU)
        pieces, res = [], g
        for i in range(G_SPLIT):
            pieces.append(res.astype(bf16))
            if i + 1 < G_SPLIT:
                res = res - pieces[i].astype(f32)
        pre = [jnp.dot(tri_g, jnp.concatenate([p[cs] for p in pieces], axis=0),
                       preferred_element_type=f32) for cs in chunks]
        qq = _fb_expand(qkl_ref[rows, :GLA_QK_W])
        kk = _fb_expand(qkl_ref[rows, GLA_QK_W:2 * GLA_QK_W])
        ke, dec = [], []
        for j, cs in enumerate(chunks):
            tot = pre[j][CHUNK - 1:CHUNK, :]
            bc = jnp.where(fwd_lane, pre[j], tot - pre[j] + g[cs])
            dec.append(jnp.exp2(tot))
            dec_s[c0 + j] = dec[j]
            c_rows = pl.ds(pl.multiple_of((c0 + j) * CHUNK, CHUNK), CHUNK)
            qd_s[c_rows, :] = (qq[cs] * jnp.exp2(bc)).astype(bf16)
            ki_s[c_rows, :] = (kk[cs] * jnp.exp2(-bc)).astype(bf16)
            ke.append((kk[cs] * jnp.exp2(tot - bc)).astype(bf16))
        v_blk = v_ref[rows, :]
        d_st = [[lax.dot_general(v_blk[cs, sl], ke[j][:, sl], (((0,), (0,)), ((), ())),
                                 preferred_element_type=f32)
                 for sl in heads] for j, cs in enumerate(chunks)]
        for h, sl in enumerate(heads):
            st = st_s[h]
            for j in range(nb):
                dst_s[(c0 + j) * GLA_HEADS + h] = d_st[j][h]
                stf_s[(c0 + j) * GLA_HEADS + h] = st.astype(bf16)
                st = st * dec[j][:, sl] + d_st[j][h]
            st_s[h] = st
        return carry

    lax.fori_loop(0, n_blocks, pass1, 0)

    st_s[...] = jnp.zeros_like(st_s)

    def pass2(i, carry):
        b = n_blocks - 1 - i
        rows = pl.ds(pl.multiple_of(b * blk, blk), blk)
        c0 = b * nb
        qd = qd_s[rows, :]
        ki = ki_s[rows, :]
        v_blk = v_ref[rows, :]
        st_c = [[None] * GLA_HEADS for _ in range(nb)]
        for h, sl in enumerate(heads):
            st = st_s[h]
            for j in reversed(range(nb)):
                idx = (c0 + j) * GLA_HEADS + h
                st_c[j][h] = jnp.where(fwd_lane_h, stf_s[idx], st.astype(bf16))
                st = st * dec_s[c0 + j][:, sl] + dst_s[idx]
            st_s[h] = st
        zero = jnp.zeros((CHUNK, LANES), bf16)

        def scores(j):
            out = []
            for sl in heads:
                qd_h = qd[chunks[j], sl]
                lhs = jnp.concatenate([jnp.where(fwd_lane_h, qd_h, zero),
                                       jnp.where(fwd_lane_h, zero, qd_h)], axis=0)
                out.append(lax.dot_general(lhs, ki[chunks[j], sl], (((1,), (1,)), ((), ())),
                                           preferred_element_type=f32))
            return out

        r_blk = r_ref[rows, :]
        gate = r_blk * (1.0 / (1.0 + jnp.exp(-r_blk))) * ng_ref[...]
        order = list(reversed(range(nb)))
        pending = {j: scores(j) for j in order[:GLA_LOOKAHEAD]}
        for pos, j in enumerate(order):
            cs = chunks[j]
            sc = pending.pop(j)
            if pos + GLA_LOOKAHEAD < nb:
                pending[order[pos + GLA_LOOKAHEAD]] = scores(order[pos + GLA_LOOKAHEAD])
            c_rows = pl.ds(pl.multiple_of((c0 + j) * CHUNK, CHUNK), CHUNK)
            for h, sl in enumerate(heads):
                a = jnp.where(tril, sc[h][:CHUNK], sc[h][CHUNK:]).astype(bf16)
                o_h = jnp.dot(a, v_blk[cs, sl], preferred_element_type=f32)
                o_h = o_h + lax.dot_general(qd[cs, sl], st_c[j][h], (((1,), (1,)), ((), ())),
                                            preferred_element_type=f32)
                o_h = o_h * lax.rsqrt(jnp.mean(o_h * o_h, axis=-1, keepdims=True) + LN_EPS)
                o_ref[c_rows, sl] = (o_h * gate[cs, sl]).astype(o_ref.dtype)
        return carry

    lax.fori_loop(0, n_blocks, pass2, 0)


def _gla(qkl, v, r, w2p, b2p, ng):
    b, l, _ = qkl.shape
    n_chunks = l // CHUNK
    seq = lambda w: pl.BlockSpec((None, l, w), lambda i: (i, 0, 0))
    const = lambda s: pl.BlockSpec(s, lambda i: (0,) * len(s))
    return pl.pallas_call(
        functools.partial(_gla_kernel, n_chunks=n_chunks),
        out_shape=jax.ShapeDtypeStruct((b, l, GLA_V_W), jnp.bfloat16),
        grid=(b,),
        in_specs=[seq(QKL_W), seq(GLA_V_W), seq(GLA_V_W),
                  const((LR_PAD, FB_W)), const((1, FB_W)), const((1, GLA_V_W))],
        out_specs=seq(GLA_V_W),
        scratch_shapes=[pltpu.VMEM((l, FB_W), jnp.bfloat16),
                        pltpu.VMEM((l, FB_W), jnp.bfloat16),
                        pltpu.VMEM((n_chunks, 1, FB_W), jnp.float32),
                        pltpu.VMEM((n_chunks * GLA_HEADS, GLA_DV, LANES), jnp.float32),
                        pltpu.VMEM((n_chunks * GLA_HEADS, GLA_DV, LANES), jnp.bfloat16),
                        pltpu.VMEM((GLA_HEADS, GLA_DV, LANES), jnp.float32)],
        compiler_params=_cparams(1),
        name="gla",
    )(qkl, v, r, w2p, b2p, ng)


def _nat_kernel(q_ref, k_ref, v_ref, bias_ref, o_ref, *, n_rows):
    f32, bf16 = jnp.float32, jnp.bfloat16
    blk_r = lax.broadcasted_iota(jnp.int32, (QUAD_W, QUAD_W), 0) // NAT_HD
    blk_c = lax.broadcasted_iota(jnp.int32, (QUAD_W, QUAD_W), 1) // NAT_HD
    blockdiag = blk_r == blk_c
    out_head = lax.broadcasted_iota(jnp.int32, (GRID_W, QUAD_W), 1) // NAT_HD

    def rows_body(i, carry):
        items = []
        for t in range(NAT_ROWS_PER_ITER):
            r = i * NAT_ROWS_PER_ITER + t
            rs = jnp.clip(r - MAX_KR // 2, 0, n_rows - MAX_KR)
            q_rows = pl.ds(pl.multiple_of(r * GRID_W, GRID_W), GRID_W)
            w_rows = pl.ds(pl.multiple_of(rs * GRID_W, GRID_W), WIN_KEYS)
            for quad in range(N_QUADS):
                items.append((q_rows, w_rows, r - rs, quad,
                              slice(quad * QUAD_W, (quad + 1) * QUAD_W)))
        def score(item):
            q_rows, w_rows, e, quad, sl = item
            q_r = q_ref[q_rows, sl]
            q_bd = jnp.where(blockdiag, jnp.concatenate([q_r] * QUAD, axis=0),
                             jnp.zeros((QUAD_W, QUAD_W), bf16))
            return lax.dot_general(q_bd, k_ref[w_rows, sl], (((1,), (1,)), ((), ())),
                                   preferred_element_type=f32)

        def attend(item, s):
            q_rows, w_rows, e, quad, sl = item
            s = s + bias_ref[e, quad]
            p = jnp.exp2(s - jnp.max(s, axis=-1, keepdims=True))
            denom = jnp.sum(p, axis=-1, keepdims=True)
            return p.astype(bf16), denom

        def finish(item, p, denom):
            q_rows, w_rows, e, quad, sl = item
            o = jnp.dot(p, v_ref[w_rows, sl], preferred_element_type=f32)
            o = o * (1.0 / denom)
            acc = jnp.zeros((GRID_W, QUAD_W), f32)
            for h in range(QUAD):
                acc = acc + jnp.where(out_head == h, o[h * GRID_W:(h + 1) * GRID_W, :], 0.0)
            o_ref[q_rows, sl] = acc.astype(o_ref.dtype)

        pending = {k: score(items[k]) for k in range(min(NAT_LOOKAHEAD, len(items)))}
        for k, item in enumerate(items):
            p, denom = attend(item, pending.pop(k))
            if k + NAT_LOOKAHEAD < len(items):
                pending[k + NAT_LOOKAHEAD] = score(items[k + NAT_LOOKAHEAD])
            finish(item, p, denom)
        return carry

    lax.fori_loop(0, n_rows // NAT_ROWS_PER_ITER, rows_body, 0)


def _nat(q, k, v, bias):
    b, l, _ = q.shape
    n_rows = l // GRID_W
    seq = pl.BlockSpec((None, l, NAT_W), lambda i: (i, 0, 0))
    return pl.pallas_call(
        functools.partial(_nat_kernel, n_rows=n_rows),
        out_shape=jax.ShapeDtypeStruct((b, l, NAT_W), jnp.bfloat16),
        grid=(b,),
        in_specs=[seq, seq, seq,
                  pl.BlockSpec(bias.shape, lambda i: (0, 0, 0, 0))],
        out_specs=seq,
        compiler_params=_cparams(1),
        name="nat",
    )(q, k, v, bias)


def _nat_bias_table(rpb):
    c = np.arange(GRID_W)[:, None]
    m = np.arange(GRID_W)[None, :]
    col_start = np.clip(c - KC // 2, 0, GRID_W - KC)
    valid = (m >= col_start) & (m < col_start + KC)
    co = np.clip(m - c + KC - 1, 0, 2 * KC - 2)
    onehot = (co[:, :, None] == np.arange(2 * KC - 1)).astype(np.float32)
    toep = jnp.einsum("hrt,cmt->hrcm", rpb * LOG2E, jnp.asarray(onehot),
                      precision=lax.Precision.HIGHEST)
    toep = jnp.where(jnp.asarray(valid)[None, None], toep, MASK_VALUE)
    t = jnp.stack([toep[:, MAX_KR - 1 - e:2 * MAX_KR - 1 - e] for e in range(MAX_KR)])
    t = jnp.transpose(t, (0, 1, 3, 2, 4))
    return t.reshape(MAX_KR, N_QUADS, QUAD * GRID_W, MAX_KR * GRID_W).astype(jnp.float32)


def _layer_norm(y, g, b):
    mu = jnp.mean(y, axis=-1, keepdims=True)
    yc = y - mu
    var = jnp.mean(yc * yc, axis=-1, keepdims=True)
    return yc * lax.rsqrt(var + LN_EPS) * g + b


def _mix_ffn_kernel(oa_ref, on_ref, x_ref, wa_ref, wn_ref, g1_ref, b1_ref,
                    w1_ref, w2_ref, g2_ref, b2_ref, o_ref):
    f32, bf16 = jnp.float32, jnp.bfloat16
    rows_per_group = x_ref.shape[0] // FFN_ROW_GROUPS
    groups = [slice(i * rows_per_group, (i + 1) * rows_per_group) for i in range(FFN_ROW_GROUPS)]
    mix = [jnp.dot(oa_ref[rows, :], wa_ref[...], preferred_element_type=f32)
           + jnp.dot(on_ref[rows, :], wn_ref[...], preferred_element_type=f32) for rows in groups]
    x1 = [_layer_norm(ALPHA * x_ref[rows, :] + m, g1_ref[...], b1_ref[...])
          for rows, m in zip(groups, mix)]
    h = [jnp.dot(x.astype(bf16), w1_ref[...], preferred_element_type=f32) for x in x1]
    act = []
    for hh in h:
        gate = hh[:, :FFN_HIDDEN]
        act.append((gate * (1.0 / (1.0 + jnp.exp(-gate))) * hh[:, FFN_HIDDEN:]).astype(bf16))
    y = [jnp.dot(a, w2_ref[...], preferred_element_type=f32) for a in act]
    for rows, x, yy in zip(groups, x1, y):
        o_ref[rows, :] = _layer_norm(ALPHA * x + yy, g2_ref[...], b2_ref[...])


def _mix_ffn(oa, on, x2d, p, tm):
    t = x2d.shape[0]
    tile = lambda w: pl.BlockSpec((tm, w), lambda i: (i, 0))
    const = lambda s: pl.BlockSpec(s, lambda i: (0, 0), pipeline_mode=pl.Buffered(1))
    vec = const((1, D_MODEL))
    return pl.pallas_call(
        _mix_ffn_kernel,
        out_shape=jax.ShapeDtypeStruct((t, D_MODEL), jnp.float32),
        grid=(t // tm,),
        in_specs=[tile(GLA_V_W), tile(NAT_W), tile(D_MODEL),
                  const((GLA_V_W, D_MODEL)), const((NAT_W, D_MODEL)), vec, vec,
                  const((D_MODEL, 2 * FFN_HIDDEN)), const((FFN_HIDDEN, D_MODEL)), vec, vec],
        out_specs=tile(D_MODEL),
        compiler_params=_cparams(1),
        name="mix_ffn",
    )(oa, on, x2d, p["wa"], p["wn"], p["ln1_g"], p["ln1_b"],
      p["w1"], p["w2"], p["ln2_g"], p["ln2_b"])


def _prep_layer(w_in, gate_w2, gate_b, norm_g, rpb, w_out, ln1_g, ln1_b,
                w_ffn_in, w_ffn_out, ln2_g, ln2_b):
    f32, bf16 = jnp.float32, jnp.bfloat16
    sizes = (GLA_QK_W, GLA_QK_W, GLA_V_W, GLA_V_W, 2 * GATE_RANK, NAT_W, NAT_W, NAT_W)
    pts = np.cumsum((0,) + sizes)
    q_a, k_a, v_a, r_a, lr, q_b, k_b, v_b = [w_in[:, pts[i]:pts[i + 1]] for i in range(8)]
    lr = jnp.pad(lr, ((0, 0), (0, LR_PAD - 2 * GATE_RANK)))
    w_perm = jnp.concatenate(
        [q_a * GLA_DK ** -0.5, k_a, lr, v_a, r_a, q_b * (NAT_HD ** -0.5 * LOG2E), k_b, v_b],
        axis=1)
    w2p = jnp.zeros((LR_PAD, GLA_HEADS, 2, GLA_DK), f32)
    w2p = w2p.at[:GATE_RANK, :, 0, :].set(gate_w2[0].reshape(GATE_RANK, GLA_HEADS, GLA_DK))
    w2p = w2p.at[GATE_RANK:2 * GATE_RANK, :, 1, :].set(
        gate_w2[1].reshape(GATE_RANK, GLA_HEADS, GLA_DK))
    b2p = jnp.stack([gate_b[0].reshape(GLA_HEADS, GLA_DK),
                     gate_b[1].reshape(GLA_HEADS, GLA_DK)], axis=1)
    return dict(
        w_perm=w_perm.astype(bf16),
        w2p=w2p.reshape(LR_PAD, FB_W).astype(bf16),
        b2p=b2p.reshape(1, FB_W).astype(f32),
        ng=norm_g.reshape(1, GLA_V_W).astype(f32),
        bias=_nat_bias_table(rpb.astype(f32)),
        wa=w_out[:GLA_V_W].astype(bf16), wn=w_out[GLA_V_W:].astype(bf16),
        ln1_g=ln1_g.reshape(1, D_MODEL), ln1_b=ln1_b.reshape(1, D_MODEL),
        w1=w_ffn_in.astype(bf16), w2=w_ffn_out.astype(bf16),
        ln2_g=ln2_g.reshape(1, D_MODEL), ln2_b=ln2_b.reshape(1, D_MODEL),
    )


def _row_tile(t, want):
    return want if t % want == 0 else t


def _layer(x2d, b, l, p):
    tm = _row_tile(x2d.shape[0], FFN_ROW_TILE)
    qkl, v_a, r_a, q_b, k_b, v_b = _inproj(x2d, p["w_perm"], _row_tile(x2d.shape[0], PROJ_ROW_TILE))
    seq = lambda a: a.reshape(b, l, a.shape[-1])
    o_a = _gla(seq(qkl), seq(v_a), seq(r_a), p["w2p"], p["b2p"], p["ng"])
    o_n = _nat(seq(q_b), seq(k_b), seq(v_b), p["bias"])
    return _mix_ffn(o_a.reshape(b * l, GLA_V_W), o_n.reshape(b * l, NAT_W), x2d, p, tm)


def _trunk(x, layers):
    b, l, d = x.shape
    x2d = x.reshape(b * l, d)
    for p in layers:
        x2d = _layer(x2d, b, l, p)
    return x2d.reshape(b, l, d)


def kernel(x_prompt, x_sample, w_in, gla_gate_w2, gla_gate_b, gla_norm_g, nat_rpb, w_out,
           ln1_g, ln1_b, w_ffn_in, w_ffn_out, ln2_g, ln2_b):
    layers = [_prep_layer(w_in[i], gla_gate_w2[i], gla_gate_b[i], gla_norm_g[i], nat_rpb[i],
                          w_out[i], ln1_g[i], ln1_b[i], w_ffn_in[i], w_ffn_out[i],
                          ln2_g[i], ln2_b[i]) for i in range(DEPTH)]
    return (_trunk(x_prompt, layers), _trunk(x_sample, layers))
```

```python
import functools

import numpy as np
import jax
import jax.numpy as jnp
from jax import lax
from jax.experimental import pallas as pl
from jax.experimental.pallas import tpu as pltpu

D_MODEL = 1024
DEPTH = 2
GLA_HEADS = 4
GLA_DK = 64
GLA_DV = 128
GLA_QK_W = GLA_HEADS * GLA_DK
GLA_V_W = GLA_HEADS * GLA_DV
GATE_RANK = 16
GATE_TAU = 16.0
CHUNK = 64
NAT_HEADS = 8
NAT_HD = 64
NAT_W = NAT_HEADS * NAT_HD
GRID_W = 64
MAX_KR = 8
KC = 16
FFN_HIDDEN = 2816
ALPHA = (2 * DEPTH) ** 0.25
LN_EPS = 1e-5
MASK_VALUE = -1e30

LANES = 128
LR_PAD = LANES
FB_W = GLA_HEADS * 2 * GLA_DK
QUAD = 4
QUAD_W = QUAD * NAT_HD
N_QUADS = NAT_HEADS // QUAD
WIN_KEYS = MAX_KR * GRID_W
GLA_BLOCK_CHUNKS = 16
GLA_LOOKAHEAD = 2
NAT_ROWS_PER_ITER = 16
NAT_LOOKAHEAD = 2
PROJ_ROW_TILE = 1024
FFN_ROW_TILE = 512
FFN_ROW_GROUPS = 2
LOG2E = 1.4426950408889634

VMEM_LIMIT = 56 * 1024 * 1024

QKL_W = 2 * GLA_QK_W + LR_PAD
PROJ_OUT_W = (QKL_W, GLA_V_W, GLA_V_W, NAT_W, NAT_W, NAT_W)
PROJ_OUT_DT = (jnp.float32, jnp.bfloat16, jnp.float32, jnp.bfloat16, jnp.bfloat16, jnp.bfloat16)
G_SPLIT = 2


def _cparams(n_grid):
    return pltpu.CompilerParams(
        dimension_semantics=("arbitrary",) * n_grid, vmem_limit_bytes=VMEM_LIMIT)


def _inproj_kernel(x_ref, w_ref, *o_refs):
    xb = x_ref[...].astype(jnp.bfloat16)
    off = 0
    for o_ref, width in zip(o_refs, PROJ_OUT_W):
        acc = jnp.dot(xb, w_ref[:, off:off + width], preferred_element_type=jnp.float32)
        o_ref[...] = acc.astype(o_ref.dtype)
        off += width


def _inproj(x2d, w_perm, tm):
    t = x2d.shape[0]
    wtot = w_perm.shape[1]
    return pl.pallas_call(
        _inproj_kernel,
        out_shape=[jax.ShapeDtypeStruct((t, w), dt) for w, dt in zip(PROJ_OUT_W, PROJ_OUT_DT)],
        grid=(t // tm,),
        in_specs=[pl.BlockSpec((tm, D_MODEL), lambda i: (i, 0)),
                  pl.BlockSpec((D_MODEL, wtot), lambda i: (0, 0))],
        out_specs=[pl.BlockSpec((tm, w), lambda i: (i, 0)) for w in PROJ_OUT_W],
        compiler_params=_cparams(1),
        name="inproj",
    )(x2d, w_perm)


def _fb_expand(x):
    lane = lax.broadcasted_iota(jnp.int32, (x.shape[0], LANES), 1)
    low = lane < GLA_DK
    slabs = []
    for p in range(GLA_HEADS // 2):
        pair = x[:, p * LANES:(p + 1) * LANES]
        swapped = pltpu.roll(pair, GLA_DK, axis=1)
        slabs.append(jnp.where(low, pair, swapped))
        slabs.append(jnp.where(low, swapped, pair))
    return jnp.concatenate(slabs, axis=1)


def _gla_kernel(qkl_ref, v_ref, r_ref, w2_ref, b2_ref, ng_ref, o_ref,
                qe_s, ke_s, dec_s, dst_s, stf_s, st_s, *, n_chunks):
    f32, bf16 = jnp.float32, jnp.bfloat16
    nb = GLA_BLOCK_CHUNKS
    n_blocks = n_chunks // nb
    blk = nb * CHUNK
    lane_fb = lax.broadcasted_iota(jnp.int32, (1, FB_W), 1)
    fwd_lane = (lane_fb % LANES) < GLA_DK
    fwd_lane_h = lax.broadcasted_iota(jnp.int32, (1, LANES), 1) < GLA_DK
    row = lax.broadcasted_iota(jnp.int32, (CHUNK, CHUNK), 0)
    col = lax.broadcasted_iota(jnp.int32, (CHUNK, CHUNK), 1)
    tril = row >= col
    tri_g = jnp.concatenate([tril.astype(bf16)] * G_SPLIT, axis=1)
    heads = [slice(h * LANES, (h + 1) * LANES) for h in range(GLA_HEADS)]
    chunks = [slice(j * CHUNK, (j + 1) * CHUNK) for j in range(nb)]

    st_s[...] = jnp.zeros_like(st_s)

    def pass1(b, carry):
        rows = pl.ds(pl.multiple_of(b * blk, blk), blk)
        c0 = b * nb
        z = jnp.dot(qkl_ref[rows, 2 * GLA_QK_W:].astype(bf16), w2_ref[...],
                    preferred_element_type=f32) + b2_ref[...]
        g = (jnp.minimum(z, 0.0) - jnp.log(1.0 + jnp.exp(-jnp.abs(z)))) * (LOG2E / GATE_TAU)
        pieces, res = [], g
        for i in range(G_SPLIT):
            pieces.append(res.astype(bf16))
            if i + 1 < G_SPLIT:
                res = res - pieces[i].astype(f32)
        pre = [jnp.dot(tri_g, jnp.concatenate([p[cs] for p in pieces], axis=0),
                       preferred_element_type=f32) for cs in chunks]
        qq = _fb_expand(qkl_ref[rows, :GLA_QK_W])
        kk = _fb_expand(qkl_ref[rows, GLA_QK_W:2 * GLA_QK_W])
        ke, dec = [], []
        for j, cs in enumerate(chunks):
            tot = pre[j][CHUNK - 1:CHUNK, :]
            bc = jnp.where(fwd_lane, pre[j], tot - pre[j] + g[cs])
            dec.append(jnp.exp2(tot))
            dec_s[c0 + j] = dec[j]
            c_rows = pl.ds(pl.multiple_of((c0 + j) * CHUNK, CHUNK), CHUNK)
            rel = bc - tot
            qe_s[c_rows, :] = (qq[cs] * jnp.exp2(rel)).astype(bf16)
            ke.append((kk[cs] * jnp.exp2(-rel)).astype(bf16))
            ke_s[c_rows, :] = ke[j]
        v_blk = v_ref[rows, :]
        d_st = [[lax.dot_general(v_blk[cs, sl], ke[j][:, sl], (((0,), (0,)), ((), ())),
                                 preferred_element_type=f32)
                 for sl in heads] for j, cs in enumerate(chunks)]
        for h, sl in enumerate(heads):
            st = st_s[h]
            for j in range(nb):
                dst_s[(c0 + j) * GLA_HEADS + h] = d_st[j][h]
                decayed = st * dec[j][:, sl]
                stf_s[(c0 + j) * GLA_HEADS + h] = decayed.astype(bf16)
                st = decayed + d_st[j][h]
            st_s[h] = st
        return carry

    lax.fori_loop(0, n_blocks, pass1, 0)

    st_s[...] = jnp.zeros_like(st_s)

    def pass2(i, carry):
        b = n_blocks - 1 - i
        rows = pl.ds(pl.multiple_of(b * blk, blk), blk)
        c0 = b * nb
        qe = qe_s[rows, :]
        ke = ke_s[rows, :]
        v_blk = v_ref[rows, :]
        st_c = [[None] * GLA_HEADS for _ in range(nb)]
        for h, sl in enumerate(heads):
            st = st_s[h]
            for j in reversed(range(nb)):
                idx = (c0 + j) * GLA_HEADS + h
                decayed = st * dec_s[c0 + j][:, sl]
                st_c[j][h] = jnp.where(fwd_lane_h, stf_s[idx], decayed.astype(bf16))
                st = decayed + dst_s[idx]
            st_s[h] = st
        zero = jnp.zeros((CHUNK, LANES), bf16)

        def scores(j):
            out = []
            for sl in heads:
                qe_h = qe[chunks[j], sl]
                lhs = jnp.concatenate([jnp.where(fwd_lane_h, qe_h, zero),
                                       jnp.where(fwd_lane_h, zero, qe_h)], axis=0)
                out.append(lax.dot_general(lhs, ke[chunks[j], sl], (((1,), (1,)), ((), ())),
                                           preferred_element_type=f32))
            return out

        r_blk = r_ref[rows, :]
        gate = r_blk * (1.0 / (1.0 + jnp.exp(-r_blk))) * ng_ref[...]
        order = list(reversed(range(nb)))
        pending = {j: scores(j) for j in order[:GLA_LOOKAHEAD]}
        for pos, j in enumerate(order):
            cs = chunks[j]
            sc = pending.pop(j)
            if pos + GLA_LOOKAHEAD < nb:
                pending[order[pos + GLA_LOOKAHEAD]] = scores(order[pos + GLA_LOOKAHEAD])
            c_rows = pl.ds(pl.multiple_of((c0 + j) * CHUNK, CHUNK), CHUNK)
            for h, sl in enumerate(heads):
                a = jnp.where(tril, sc[h][:CHUNK], sc[h][CHUNK:]).astype(bf16)
                o_h = jnp.dot(a, v_blk[cs, sl], preferred_element_type=f32)
                o_h = o_h + lax.dot_general(qe[cs, sl], st_c[j][h], (((1,), (1,)), ((), ())),
                                            preferred_element_type=f32)
                o_h = o_h * lax.rsqrt(jnp.mean(o_h * o_h, axis=-1, keepdims=True) + LN_EPS)
                o_ref[c_rows, sl] = (o_h * gate[cs, sl]).astype(o_ref.dtype)
        return carry

    lax.fori_loop(0, n_blocks, pass2, 0)


def _gla(qkl, v, r, w2p, b2p, ng):
    b, l, _ = qkl.shape
    n_chunks = l // CHUNK
    seq = lambda w: pl.BlockSpec((None, l, w), lambda i: (i, 0, 0))
    const = lambda s: pl.BlockSpec(s, lambda i: (0,) * len(s))
    return pl.pallas_call(
        functools.partial(_gla_kernel, n_chunks=n_chunks),
        out_shape=jax.ShapeDtypeStruct((b, l, GLA_V_W), jnp.bfloat16),
        grid=(b,),
        in_specs=[seq(QKL_W), seq(GLA_V_W), seq(GLA_V_W),
                  const((LR_PAD, FB_W)), const((1, FB_W)), const((1, GLA_V_W))],
        out_specs=seq(GLA_V_W),
        scratch_shapes=[pltpu.VMEM((l, FB_W), jnp.bfloat16),
                        pltpu.VMEM((l, FB_W), jnp.bfloat16),
                        pltpu.VMEM((n_chunks, 1, FB_W), jnp.float32),
                        pltpu.VMEM((n_chunks * GLA_HEADS, GLA_DV, LANES), jnp.float32),
                        pltpu.VMEM((n_chunks * GLA_HEADS, GLA_DV, LANES), jnp.bfloat16),
                        pltpu.VMEM((GLA_HEADS, GLA_DV, LANES), jnp.float32)],
        compiler_params=_cparams(1),
        name="gla",
    )(qkl, v, r, w2p, b2p, ng)


def _nat_kernel(q_ref, k_ref, v_ref, bias_ref, o_ref, *, n_rows):
    f32, bf16 = jnp.float32, jnp.bfloat16
    blk_r = lax.broadcasted_iota(jnp.int32, (QUAD_W, QUAD_W), 0) // NAT_HD
    blk_c = lax.broadcasted_iota(jnp.int32, (QUAD_W, QUAD_W), 1) // NAT_HD
    blockdiag = blk_r == blk_c
    out_head = lax.broadcasted_iota(jnp.int32, (GRID_W, QUAD_W), 1) // NAT_HD

    def rows_body(i, carry):
        items = []
        for t in range(NAT_ROWS_PER_ITER):
            r = i * NAT_ROWS_PER_ITER + t
            rs = jnp.clip(r - MAX_KR // 2, 0, n_rows - MAX_KR)
            q_rows = pl.ds(pl.multiple_of(r * GRID_W, GRID_W), GRID_W)
            w_rows = pl.ds(pl.multiple_of(rs * GRID_W, GRID_W), WIN_KEYS)
            for quad in range(N_QUADS):
                items.append((q_rows, w_rows, r - rs, quad,
                              slice(quad * QUAD_W, (quad + 1) * QUAD_W)))
        def score(item):
            q_rows, w_rows, e, quad, sl = item
            q_r = q_ref[q_rows, sl]
            q_bd = jnp.where(blockdiag, jnp.concatenate([q_r] * QUAD, axis=0),
                             jnp.zeros((QUAD_W, QUAD_W), bf16))
            return lax.dot_general(q_bd, k_ref[w_rows, sl], (((1,), (1,)), ((), ())),
                                   preferred_element_type=f32)

        def attend(item, s):
            q_rows, w_rows, e, quad, sl = item
            s = s + bias_ref[e, quad]
            p = jnp.exp2(s - jnp.max(s, axis=-1, keepdims=True))
            denom = jnp.sum(p, axis=-1, keepdims=True)
            return p.astype(bf16), denom

        def finish(item, p, denom):
            q_rows, w_rows, e, quad, sl = item
            o = jnp.dot(p, v_ref[w_rows, sl], preferred_element_type=f32)
            o = o * (1.0 / denom)
            acc = jnp.zeros((GRID_W, QUAD_W), f32)
            for h in range(QUAD):
                acc = acc + jnp.where(out_head == h, o[h * GRID_W:(h + 1) * GRID_W, :], 0.0)
            o_ref[q_rows, sl] = acc.astype(o_ref.dtype)

        pending = {k: score(items[k]) for k in range(min(NAT_LOOKAHEAD, len(items)))}
        for k, item in enumerate(items):
            p, denom = attend(item, pending.pop(k))
            if k + NAT_LOOKAHEAD < len(items):
                pending[k + NAT_LOOKAHEAD] = score(items[k + NAT_LOOKAHEAD])
            finish(item, p, denom)
        return carry

    lax.fori_loop(0, n_rows // NAT_ROWS_PER_ITER, rows_body, 0)


def _nat(q, k, v, bias):
    b, l, _ = q.shape
    n_rows = l // GRID_W
    seq = pl.BlockSpec((None, l, NAT_W), lambda i: (i, 0, 0))
    return pl.pallas_call(
        functools.partial(_nat_kernel, n_rows=n_rows),
        out_shape=jax.ShapeDtypeStruct((b, l, NAT_W), jnp.bfloat16),
        grid=(b,),
        in_specs=[seq, seq, seq,
                  pl.BlockSpec(bias.shape, lambda i: (0, 0, 0, 0))],
        out_specs=seq,
        compiler_params=_cparams(1),
        name="nat",
    )(q, k, v, bias)


def _nat_bias_table(rpb):
    c = np.arange(GRID_W)[:, None]
    m = np.arange(GRID_W)[None, :]
    col_start = np.clip(c - KC // 2, 0, GRID_W - KC)
    valid = (m >= col_start) & (m < col_start + KC)
    co = np.clip(m - c + KC - 1, 0, 2 * KC - 2)
    onehot = (co[:, :, None] == np.arange(2 * KC - 1)).astype(np.float32)
    toep = jnp.einsum("hrt,cmt->hrcm", rpb * LOG2E, jnp.asarray(onehot),
                      precision=lax.Precision.HIGHEST)
    toep = jnp.where(jnp.asarray(valid)[None, None], toep, MASK_VALUE)
    t = jnp.stack([toep[:, MAX_KR - 1 - e:2 * MAX_KR - 1 - e] for e in range(MAX_KR)])
    t = jnp.transpose(t, (0, 1, 3, 2, 4))
    return t.reshape(MAX_KR, N_QUADS, QUAD * GRID_W, MAX_KR * GRID_W).astype(jnp.float32)


def _layer_norm(y, g, b):
    mu = jnp.mean(y, axis=-1, keepdims=True)
    yc = y - mu
    var = jnp.mean(yc * yc, axis=-1, keepdims=True)
    return yc * lax.rsqrt(var + LN_EPS) * g + b


def _mix_ffn_kernel(oa_ref, on_ref, x_ref, wa_ref, wn_ref, g1_ref, b1_ref,
                    w1_ref, w2_ref, g2_ref, b2_ref, o_ref):
    f32, bf16 = jnp.float32, jnp.bfloat16
    rows_per_group = x_ref.shape[0] // FFN_ROW_GROUPS
    groups = [slice(i * rows_per_group, (i + 1) * rows_per_group) for i in range(FFN_ROW_GROUPS)]
    mix = [jnp.dot(oa_ref[rows, :], wa_ref[...], preferred_element_type=f32)
           + jnp.dot(on_ref[rows, :], wn_ref[...], preferred_element_type=f32) for rows in groups]
    x1 = [_layer_norm(ALPHA * x_ref[rows, :] + m, g1_ref[...], b1_ref[...])
          for rows, m in zip(groups, mix)]
    h = [jnp.dot(x.astype(bf16), w1_ref[...], preferred_element_type=f32) for x in x1]
    act = []
    for hh in h:
        gate = hh[:, :FFN_HIDDEN]
        act.append((gate * (1.0 / (1.0 + jnp.exp(-gate))) * hh[:, FFN_HIDDEN:]).astype(bf16))
    y = [jnp.dot(a, w2_ref[...], preferred_element_type=f32) for a in act]
    for rows, x, yy in zip(groups, x1, y):
        o_ref[rows, :] = _layer_norm(ALPHA * x + yy, g2_ref[...], b2_ref[...])


def _mix_ffn(oa, on, x2d, p, tm):
    t = x2d.shape[0]
    tile = lambda w: pl.BlockSpec((tm, w), lambda i: (i, 0))
    const = lambda s: pl.BlockSpec(s, lambda i: (0, 0), pipeline_mode=pl.Buffered(1))
    vec = const((1, D_MODEL))
    return pl.pallas_call(
        _mix_ffn_kernel,
        out_shape=jax.ShapeDtypeStruct((t, D_MODEL), jnp.float32),
        grid=(t // tm,),
        in_specs=[tile(GLA_V_W), tile(NAT_W), tile(D_MODEL),
                  const((GLA_V_W, D_MODEL)), const((NAT_W, D_MODEL)), vec, vec,
                  const((D_MODEL, 2 * FFN_HIDDEN)), const((FFN_HIDDEN, D_MODEL)), vec, vec],
        out_specs=tile(D_MODEL),
        compiler_params=_cparams(1),
        name="mix_ffn",
    )(oa, on, x2d, p["wa"], p["wn"], p["ln1_g"], p["ln1_b"],
      p["w1"], p["w2"], p["ln2_g"], p["ln2_b"])


def _prep_layer(w_in, gate_w2, gate_b, norm_g, rpb, w_out, ln1_g, ln1_b,
                w_ffn_in, w_ffn_out, ln2_g, ln2_b):
    f32, bf16 = jnp.float32, jnp.bfloat16
    sizes = (GLA_QK_W, GLA_QK_W, GLA_V_W, GLA_V_W, 2 * GATE_RANK, NAT_W, NAT_W, NAT_W)
    pts = np.cumsum((0,) + sizes)
    q_a, k_a, v_a, r_a, lr, q_b, k_b, v_b = [w_in[:, pts[i]:pts[i + 1]] for i in range(8)]
    lr = jnp.pad(lr, ((0, 0), (0, LR_PAD - 2 * GATE_RANK)))
    w_perm = jnp.concatenate(
        [q_a * GLA_DK ** -0.5, k_a, lr, v_a, r_a, q_b * (NAT_HD ** -0.5 * LOG2E), k_b, v_b],
        axis=1)
    w2p = jnp.zeros((LR_PAD, GLA_HEADS, 2, GLA_DK), f32)
    w2p = w2p.at[:GATE_RANK, :, 0, :].set(gate_w2[0].reshape(GATE_RANK, GLA_HEADS, GLA_DK))
    w2p = w2p.at[GATE_RANK:2 * GATE_RANK, :, 1, :].set(
        gate_w2[1].reshape(GATE_RANK, GLA_HEADS, GLA_DK))
    b2p = jnp.stack([gate_b[0].reshape(GLA_HEADS, GLA_DK),
                     gate_b[1].reshape(GLA_HEADS, GLA_DK)], axis=1)
    return dict(
        w_perm=w_perm.astype(bf16),
        w2p=w2p.reshape(LR_PAD, FB_W).astype(bf16),
        b2p=b2p.reshape(1, FB_W).astype(f32),
        ng=norm_g.reshape(1, GLA_V_W).astype(f32),
        bias=_nat_bias_table(rpb.astype(f32)),
        wa=w_out[:GLA_V_W].astype(bf16), wn=w_out[GLA_V_W:].astype(bf16),
        ln1_g=ln1_g.reshape(1, D_MODEL), ln1_b=ln1_b.reshape(1, D_MODEL),
        w1=w_ffn_in.astype(bf16), w2=w_ffn_out.astype(bf16),
        ln2_g=ln2_g.reshape(1, D_MODEL), ln2_b=ln2_b.reshape(1, D_MODEL),
    )


def _row_tile(t, want):
    return want if t % want == 0 else t


def _layer(x2d, b, l, p):
    tm = _row_tile(x2d.shape[0], FFN_ROW_TILE)
    qkl, v_a, r_a, q_b, k_b, v_b = _inproj(x2d, p["w_perm"], _row_tile(x2d.shape[0], PROJ_ROW_TILE))
    seq = lambda a: a.reshape(b, l, a.shape[-1])
    o_a = _gla(seq(qkl), seq(v_a), seq(r_a), p["w2p"], p["b2p"], p["ng"])
    o_n = _nat(seq(q_b), seq(k_b), seq(v_b), p["bias"])
    return _mix_ffn(o_a.reshape(b * l, GLA_V_W), o_n.reshape(b * l, NAT_W), x2d, p, tm)


def _trunk(x, layers):
    b, l, d = x.shape
    x2d = x.reshape(b * l, d)
    for p in layers:
        x2d = _layer(x2d, b, l, p)
    return x2d.reshape(b, l, d)


def kernel(x_prompt, x_sample, w_in, gla_gate_w2, gla_gate_b, gla_norm_g, nat_rpb, w_out,
           ln1_g, ln1_b, w_ffn_in, w_ffn_out, ln2_g, ln2_b):
    layers = [_prep_layer(w_in[i], gla_gate_w2[i], gla_gate_b[i], gla_norm_g[i], nat_rpb[i],
                          w_out[i], ln1_g[i], ln1_b[i], w_ffn_in[i], w_ffn_out[i],
                          ln2_g[i], ln2_b[i]) for i in range(DEPTH)]
    return (_trunk(x_prompt, layers), _trunk(x_sample, layers))
```

```python
import functools

import numpy as np
import jax
import jax.numpy as jnp
from jax import lax
from jax.experimental import pallas as pl
from jax.experimental.pallas import tpu as pltpu

D_MODEL = 1024
DEPTH = 2
GLA_HEADS = 4
GLA_DK = 64
GLA_DV = 128
GLA_QK_W = GLA_HEADS * GLA_DK
GLA_V_W = GLA_HEADS * GLA_DV
GATE_RANK = 16
GATE_TAU = 16.0
CHUNK = 64
NAT_HEADS = 8
NAT_HD = 64
NAT_W = NAT_HEADS * NAT_HD
GRID_W = 64
MAX_KR = 8
KC = 16
FFN_HIDDEN = 2816
ALPHA = (2 * DEPTH) ** 0.25
LN_EPS = 1e-5
MASK_VALUE = -1e30

LANES = 128
LR_PAD = LANES
FB_W = GLA_HEADS * 2 * GLA_DK
QUAD = 4
QUAD_W = QUAD * NAT_HD
N_QUADS = NAT_HEADS // QUAD
WIN_KEYS = MAX_KR * GRID_W
GLA_BLOCK_CHUNKS = 32
GLA_LOOKAHEAD = 2
NAT_ROWS_PER_ITER = 16
NAT_LOOKAHEAD = 2
PROJ_ROW_TILE = 1024
FFN_ROW_TILE = 512
FFN_ROW_GROUPS = 2
LOG2E = 1.4426950408889634

VMEM_LIMIT = 56 * 1024 * 1024

QKL_W = 2 * GLA_QK_W + LR_PAD
PROJ_OUT_W = (QKL_W, GLA_V_W, GLA_V_W, NAT_W, NAT_W, NAT_W)
PROJ_OUT_DT = (jnp.float32, jnp.bfloat16, jnp.float32, jnp.bfloat16, jnp.bfloat16, jnp.bfloat16)
G_SPLIT = 2


def _cparams(n_grid):
    return pltpu.CompilerParams(
        dimension_semantics=("arbitrary",) * n_grid, vmem_limit_bytes=VMEM_LIMIT)


def _inproj_kernel(x_ref, w_ref, *o_refs):
    xb = x_ref[...].astype(jnp.bfloat16)
    off = 0
    for o_ref, width in zip(o_refs, PROJ_OUT_W):
        acc = jnp.dot(xb, w_ref[:, off:off + width], preferred_element_type=jnp.float32)
        o_ref[...] = acc.astype(o_ref.dtype)
        off += width


def _inproj(x2d, w_perm, tm):
    t = x2d.shape[0]
    wtot = w_perm.shape[1]
    return pl.pallas_call(
        _inproj_kernel,
        out_shape=[jax.ShapeDtypeStruct((t, w), dt) for w, dt in zip(PROJ_OUT_W, PROJ_OUT_DT)],
        grid=(t // tm,),
        in_specs=[pl.BlockSpec((tm, D_MODEL), lambda i: (i, 0)),
                  pl.BlockSpec((D_MODEL, wtot), lambda i: (0, 0))],
        out_specs=[pl.BlockSpec((tm, w), lambda i: (i, 0)) for w in PROJ_OUT_W],
        compiler_params=_cparams(1),
        name="inproj",
    )(x2d, w_perm)


def _fb_expand(x):
    lane = lax.broadcasted_iota(jnp.int32, (x.shape[0], LANES), 1)
    low = lane < GLA_DK
    slabs = []
    for p in range(GLA_HEADS // 2):
        pair = x[:, p * LANES:(p + 1) * LANES]
        swapped = pltpu.roll(pair, GLA_DK, axis=1)
        slabs.append(jnp.where(low, pair, swapped))
        slabs.append(jnp.where(low, swapped, pair))
    return jnp.concatenate(slabs, axis=1)


def _gla_kernel(qkl_ref, v_ref, r_ref, w2_ref, b2_ref, ng_ref, o_ref,
                qe_s, ke_s, dec_s, dst_s, stf_s, st_s, *, n_chunks):
    f32, bf16 = jnp.float32, jnp.bfloat16
    nb = GLA_BLOCK_CHUNKS
    n_blocks = n_chunks // nb
    blk = nb * CHUNK
    lane_fb = lax.broadcasted_iota(jnp.int32, (1, FB_W), 1)
    fwd_lane = (lane_fb % LANES) < GLA_DK
    fwd_lane_h = lax.broadcasted_iota(jnp.int32, (1, LANES), 1) < GLA_DK
    row = lax.broadcasted_iota(jnp.int32, (CHUNK, CHUNK), 0)
    col = lax.broadcasted_iota(jnp.int32, (CHUNK, CHUNK), 1)
    tril = row >= col
    tri_g = jnp.concatenate([tril.astype(bf16)] * G_SPLIT, axis=1)
    heads = [slice(h * LANES, (h + 1) * LANES) for h in range(GLA_HEADS)]
    chunks = [slice(j * CHUNK, (j + 1) * CHUNK) for j in range(nb)]

    st_s[...] = jnp.zeros_like(st_s)

    def pass1(b, carry):
        rows = pl.ds(pl.multiple_of(b * blk, blk), blk)
        c0 = b * nb
        z = jnp.dot(qkl_ref[rows, 2 * GLA_QK_W:].astype(bf16), w2_ref[...],
                    preferred_element_type=f32) + b2_ref[...]
        g = (jnp.minimum(z, 0.0) - jnp.log(1.0 + jnp.exp2(jnp.abs(z) * -LOG2E))) * (LOG2E / GATE_TAU)
        pieces, res = [], g
        for i in range(G_SPLIT):
            pieces.append(res.astype(bf16))
            if i + 1 < G_SPLIT:
                res = res - pieces[i].astype(f32)
        pre = [jnp.dot(tri_g, jnp.concatenate([p[cs] for p in pieces], axis=0),
                       preferred_element_type=f32) for cs in chunks]
        ke, dec = [], []
        for j, cs in enumerate(chunks):
            tot = pre[j][CHUNK - 1:CHUNK, :]
            dec.append(jnp.exp2(tot))
            dec_s[c0 + j] = dec[j]
            c_rows = pl.ds(pl.multiple_of((c0 + j) * CHUNK, CHUNK), CHUNK)
            edge = jnp.where(fwd_lane, tot - pre[j], pre[j] - g[cs])
            qq = _fb_expand(qkl_ref[c_rows, :GLA_QK_W])
            kk = _fb_expand(qkl_ref[c_rows, GLA_QK_W:2 * GLA_QK_W])
            qe_s[c_rows, :] = (qq * jnp.exp2(-edge)).astype(bf16)
            ke.append((kk * jnp.exp2(edge)).astype(bf16))
            ke_s[c_rows, :] = ke[j]
        d_st = [[lax.dot_general(v_ref[pl.ds(pl.multiple_of((c0 + j) * CHUNK, CHUNK), CHUNK), sl],
                                 ke[j][:, sl], (((0,), (0,)), ((), ())),
                                 preferred_element_type=f32)
                 for sl in heads] for j in range(nb)]
        for h, sl in enumerate(heads):
            st = st_s[h]
            for j in range(nb):
                dst_s[(c0 + j) * GLA_HEADS + h] = d_st[j][h]
                decayed = st * dec[j][:, sl]
                stf_s[(c0 + j) * GLA_HEADS + h] = decayed.astype(bf16)
                st = decayed + d_st[j][h]
            st_s[h] = st
        return carry

    lax.fori_loop(0, n_blocks, pass1, 0)

    st_s[...] = jnp.zeros_like(st_s)

    def pass2(i, carry):
        b = n_blocks - 1 - i
        rows = pl.ds(pl.multiple_of(b * blk, blk), blk)
        c0 = b * nb
        qe = qe_s[rows, :]
        ke = ke_s[rows, :]
        v_blk = v_ref[rows, :]
        st_c = [[None] * GLA_HEADS for _ in range(nb)]
        for h, sl in enumerate(heads):
            st = st_s[h]
            for j in reversed(range(nb)):
                idx = (c0 + j) * GLA_HEADS + h
                decayed = st * dec_s[c0 + j][:, sl]
                st_c[j][h] = jnp.where(fwd_lane_h, stf_s[idx], decayed.astype(bf16))
                st = decayed + dst_s[idx]
            st_s[h] = st
        zero = jnp.zeros((CHUNK, LANES), bf16)

        def scores(j):
            out = []
            for sl in heads:
                qe_h = qe[chunks[j], sl]
                lhs = jnp.concatenate([jnp.where(fwd_lane_h, qe_h, zero),
                                       jnp.where(fwd_lane_h, zero, qe_h)], axis=0)
                out.append(lax.dot_general(lhs, ke[chunks[j], sl], (((1,), (1,)), ((), ())),
                                           preferred_element_type=f32))
            return out

        r_blk = r_ref[rows, :]
        gate = r_blk * (1.0 / (1.0 + jnp.exp(-r_blk))) * ng_ref[...]
        order = list(reversed(range(nb)))
        pending = {j: scores(j) for j in order[:GLA_LOOKAHEAD]}
        for pos, j in enumerate(order):
            cs = chunks[j]
            sc = pending.pop(j)
            if pos + GLA_LOOKAHEAD < nb:
                pending[order[pos + GLA_LOOKAHEAD]] = scores(order[pos + GLA_LOOKAHEAD])
            c_rows = pl.ds(pl.multiple_of((c0 + j) * CHUNK, CHUNK), CHUNK)
            for h, sl in enumerate(heads):
                a = jnp.where(tril, sc[h][:CHUNK], sc[h][CHUNK:]).astype(bf16)
                o_h = jnp.dot(a, v_blk[cs, sl], preferred_element_type=f32)
                o_h = o_h + lax.dot_general(qe[cs, sl], st_c[j][h], (((1,), (1,)), ((), ())),
                                            preferred_element_type=f32)
                o_h = o_h * lax.rsqrt(jnp.mean(o_h * o_h, axis=-1, keepdims=True) + LN_EPS)
                o_ref[c_rows, sl] = (o_h * gate[cs, sl]).astype(o_ref.dtype)
        return carry

    lax.fori_loop(0, n_blocks, pass2, 0)


def _gla(qkl, v, r, w2p, b2p, ng):
    b, l, _ = qkl.shape
    n_chunks = l // CHUNK
    seq = lambda w: pl.BlockSpec((None, l, w), lambda i: (i, 0, 0))
    const = lambda s: pl.BlockSpec(s, lambda i: (0,) * len(s))
    return pl.pallas_call(
        functools.partial(_gla_kernel, n_chunks=n_chunks),
        out_shape=jax.ShapeDtypeStruct((b, l, GLA_V_W), jnp.bfloat16),
        grid=(b,),
        in_specs=[seq(QKL_W), seq(GLA_V_W), seq(GLA_V_W),
                  const((LR_PAD, FB_W)), const((1, FB_W)), const((1, GLA_V_W))],
        out_specs=seq(GLA_V_W),
        scratch_shapes=[pltpu.VMEM((l, FB_W), jnp.bfloat16),
                        pltpu.VMEM((l, FB_W), jnp.bfloat16),
                        pltpu.VMEM((n_chunks, 1, FB_W), jnp.float32),
                        pltpu.VMEM((n_chunks * GLA_HEADS, GLA_DV, LANES), jnp.float32),
                        pltpu.VMEM((n_chunks * GLA_HEADS, GLA_DV, LANES), jnp.bfloat16),
                        pltpu.VMEM((GLA_HEADS, GLA_DV, LANES), jnp.float32)],
        compiler_params=_cparams(1),
        name="gla",
    )(qkl, v, r, w2p, b2p, ng)


def _nat_kernel(q_ref, k_ref, v_ref, bias_ref, o_ref, *, n_rows):
    f32, bf16 = jnp.float32, jnp.bfloat16
    blk_r = lax.broadcasted_iota(jnp.int32, (QUAD_W, QUAD_W), 0) // NAT_HD
    blk_c = lax.broadcasted_iota(jnp.int32, (QUAD_W, QUAD_W), 1) // NAT_HD
    blockdiag = blk_r == blk_c
    out_head = lax.broadcasted_iota(jnp.int32, (GRID_W, QUAD_W), 1) // NAT_HD

    def rows_body(i, carry):
        items = []
        for t in range(NAT_ROWS_PER_ITER):
            r = i * NAT_ROWS_PER_ITER + t
            rs = jnp.clip(r - MAX_KR // 2, 0, n_rows - MAX_KR)
            q_rows = pl.ds(pl.multiple_of(r * GRID_W, GRID_W), GRID_W)
            w_rows = pl.ds(pl.multiple_of(rs * GRID_W, GRID_W), WIN_KEYS)
            for quad in range(N_QUADS):
                items.append((q_rows, w_rows, r - rs, quad,
                              slice(quad * QUAD_W, (quad + 1) * QUAD_W)))
        def score(item):
            q_rows, w_rows, e, quad, sl = item
            q_r = q_ref[q_rows, sl]
            q_bd = jnp.where(blockdiag, jnp.concatenate([q_r] * QUAD, axis=0),
                             jnp.zeros((QUAD_W, QUAD_W), bf16))
            return lax.dot_general(q_bd, k_ref[w_rows, sl], (((1,), (1,)), ((), ())),
                                   preferred_element_type=f32)

        def attend(item, s):
            q_rows, w_rows, e, quad, sl = item
            s = s + bias_ref[e, quad]
            p = jnp.exp2(s - jnp.max(s, axis=-1, keepdims=True))
            denom = jnp.sum(p, axis=-1, keepdims=True)
            return p.astype(bf16), denom

        def finish(item, p, denom):
            q_rows, w_rows, e, quad, sl = item
            o = jnp.dot(p, v_ref[w_rows, sl], preferred_element_type=f32)
            o = o * (1.0 / denom)
            acc = jnp.zeros((GRID_W, QUAD_W), f32)
            for h in range(QUAD):
                acc = acc + jnp.where(out_head == h, o[h * GRID_W:(h + 1) * GRID_W, :], 0.0)
            o_ref[q_rows, sl] = acc.astype(o_ref.dtype)

        pending = {k: score(items[k]) for k in range(min(NAT_LOOKAHEAD, len(items)))}
        for k, item in enumerate(items):
            p, denom = attend(item, pending.pop(k))
            if k + NAT_LOOKAHEAD < len(items):
                pending[k + NAT_LOOKAHEAD] = score(items[k + NAT_LOOKAHEAD])
            finish(item, p, denom)
        return carry

    lax.fori_loop(0, n_rows // NAT_ROWS_PER_ITER, rows_body, 0)


def _nat(q, k, v, bias):
    b, l, _ = q.shape
    n_rows = l // GRID_W
    seq = pl.BlockSpec((None, l, NAT_W), lambda i: (i, 0, 0))
    return pl.pallas_call(
        functools.partial(_nat_kernel, n_rows=n_rows),
        out_shape=jax.ShapeDtypeStruct((b, l, NAT_W), jnp.bfloat16),
        grid=(b,),
        in_specs=[seq, seq, seq,
                  pl.BlockSpec(bias.shape, lambda i: (0, 0, 0, 0))],
        out_specs=seq,
        compiler_params=_cparams(1),
        name="nat",
    )(q, k, v, bias)


def _nat_bias_table(rpb):
    c = np.arange(GRID_W)[:, None]
    m = np.arange(GRID_W)[None, :]
    col_start = np.clip(c - KC // 2, 0, GRID_W - KC)
    valid = (m >= col_start) & (m < col_start + KC)
    co = np.clip(m - c + KC - 1, 0, 2 * KC - 2)
    onehot = (co[:, :, None] == np.arange(2 * KC - 1)).astype(np.float32)
    toep = jnp.einsum("hrt,cmt->hrcm", rpb * LOG2E, jnp.asarray(onehot),
                      precision=lax.Precision.HIGHEST)
    toep = jnp.where(jnp.asarray(valid)[None, None], toep, MASK_VALUE)
    t = jnp.stack([toep[:, MAX_KR - 1 - e:2 * MAX_KR - 1 - e] for e in range(MAX_KR)])
    t = jnp.transpose(t, (0, 1, 3, 2, 4))
    return t.reshape(MAX_KR, N_QUADS, QUAD * GRID_W, MAX_KR * GRID_W).astype(jnp.float32)


def _layer_norm(y, g, b):
    mu = jnp.mean(y, axis=-1, keepdims=True)
    yc = y - mu
    var = jnp.mean(yc * yc, axis=-1, keepdims=True)
    return yc * lax.rsqrt(var + LN_EPS) * g + b


def _mix_ffn_kernel(oa_ref, on_ref, x_ref, wa_ref, wn_ref, g1_ref, b1_ref,
                    w1_ref, w2_ref, g2_ref, b2_ref, o_ref):
    f32, bf16 = jnp.float32, jnp.bfloat16
    rows_per_group = x_ref.shape[0] // FFN_ROW_GROUPS
    groups = [slice(i * rows_per_group, (i + 1) * rows_per_group) for i in range(FFN_ROW_GROUPS)]
    mix = [jnp.dot(oa_ref[rows, :], wa_ref[...], preferred_element_type=f32)
           + jnp.dot(on_ref[rows, :], wn_ref[...], preferred_element_type=f32) for rows in groups]
    x1 = [_layer_norm(ALPHA * x_ref[rows, :] + m, g1_ref[...], b1_ref[...])
          for rows, m in zip(groups, mix)]
    h = [jnp.dot(x.astype(bf16), w1_ref[...], preferred_element_type=f32) for x in x1]
    act = []
    for hh in h:
        gate = hh[:, :FFN_HIDDEN]
        act.append((gate * (1.0 / (1.0 + jnp.exp(-gate))) * hh[:, FFN_HIDDEN:]).astype(bf16))
    y = [jnp.dot(a, w2_ref[...], preferred_element_type=f32) for a in act]
    for rows, x, yy in zip(groups, x1, y):
        o_ref[rows, :] = _layer_norm(ALPHA * x + yy, g2_ref[...], b2_ref[...])


def _mix_ffn(oa, on, x2d, p, tm):
    t = x2d.shape[0]
    tile = lambda w: pl.BlockSpec((tm, w), lambda i: (i, 0))
    const = lambda s: pl.BlockSpec(s, lambda i: (0, 0), pipeline_mode=pl.Buffered(1))
    vec = const((1, D_MODEL))
    return pl.pallas_call(
        _mix_ffn_kernel,
        out_shape=jax.ShapeDtypeStruct((t, D_MODEL), jnp.float32),
        grid=(t // tm,),
        in_specs=[tile(GLA_V_W), tile(NAT_W), tile(D_MODEL),
                  const((GLA_V_W, D_MODEL)), const((NAT_W, D_MODEL)), vec, vec,
                  const((D_MODEL, 2 * FFN_HIDDEN)), const((FFN_HIDDEN, D_MODEL)), vec, vec],
        out_specs=tile(D_MODEL),
        compiler_params=_cparams(1),
        name="mix_ffn",
    )(oa, on, x2d, p["wa"], p["wn"], p["ln1_g"], p["ln1_b"],
      p["w1"], p["w2"], p["ln2_g"], p["ln2_b"])


def _prep_layer(w_in, gate_w2, gate_b, norm_g, rpb, w_out, ln1_g, ln1_b,
                w_ffn_in, w_ffn_out, ln2_g, ln2_b):
    f32, bf16 = jnp.float32, jnp.bfloat16
    sizes = (GLA_QK_W, GLA_QK_W, GLA_V_W, GLA_V_W, 2 * GATE_RANK, NAT_W, NAT_W, NAT_W)
    pts = np.cumsum((0,) + sizes)
    q_a, k_a, v_a, r_a, lr, q_b, k_b, v_b = [w_in[:, pts[i]:pts[i + 1]] for i in range(8)]
    lr = jnp.pad(lr, ((0, 0), (0, LR_PAD - 2 * GATE_RANK)))
    w_perm = jnp.concatenate(
        [q_a * GLA_DK ** -0.5, k_a, lr, v_a, r_a, q_b * (NAT_HD ** -0.5 * LOG2E), k_b, v_b],
        axis=1)
    w2p = jnp.zeros((LR_PAD, GLA_HEADS, 2, GLA_DK), f32)
    w2p = w2p.at[:GATE_RANK, :, 0, :].set(gate_w2[0].reshape(GATE_RANK, GLA_HEADS, GLA_DK))
    w2p = w2p.at[GATE_RANK:2 * GATE_RANK, :, 1, :].set(
        gate_w2[1].reshape(GATE_RANK, GLA_HEADS, GLA_DK))
    b2p = jnp.stack([gate_b[0].reshape(GLA_HEADS, GLA_DK),
                     gate_b[1].reshape(GLA_HEADS, GLA_DK)], axis=1)
    return dict(
        w_perm=w_perm.astype(bf16),
        w2p=w2p.reshape(LR_PAD, FB_W).astype(bf16),
        b2p=b2p.reshape(1, FB_W).astype(f32),
        ng=norm_g.reshape(1, GLA_V_W).astype(f32),
        bias=_nat_bias_table(rpb.astype(f32)),
        wa=w_out[:GLA_V_W].astype(bf16), wn=w_out[GLA_V_W:].astype(bf16),
        ln1_g=ln1_g.reshape(1, D_MODEL), ln1_b=ln1_b.reshape(1, D_MODEL),
        w1=w_ffn_in.astype(bf16), w2=w_ffn_out.astype(bf16),
        ln2_g=ln2_g.reshape(1, D_MODEL), ln2_b=ln2_b.reshape(1, D_MODEL),
    )


def _row_tile(t, want):
    return want if t % want == 0 else t


def _layer(x2d, b, l, p):
    tm = _row_tile(x2d.shape[0], FFN_ROW_TILE)
    qkl, v_a, r_a, q_b, k_b, v_b = _inproj(x2d, p["w_perm"], _row_tile(x2d.shape[0], PROJ_ROW_TILE))
    seq = lambda a: a.reshape(b, l, a.shape[-1])
    o_a = _gla(seq(qkl), seq(v_a), seq(r_a), p["w2p"], p["b2p"], p["ng"])
    o_n = _nat(seq(q_b), seq(k_b), seq(v_b), p["bias"])
    return _mix_ffn(o_a.reshape(b * l, GLA_V_W), o_n.reshape(b * l, NAT_W), x2d, p, tm)


def _trunk(x, layers):
    b, l, d = x.shape
    x2d = x.reshape(b * l, d)
    for p in layers:
        x2d = _layer(x2d, b, l, p)
    return x2d.reshape(b, l, d)


def kernel(x_prompt, x_sample, w_in, gla_gate_w2, gla_gate_b, gla_norm_g, nat_rpb, w_out,
           ln1_g, ln1_b, w_ffn_in, w_ffn_out, ln2_g, ln2_b):
    layers = [_prep_layer(w_in[i], gla_gate_w2[i], gla_gate_b[i], gla_norm_g[i], nat_rpb[i],
                          w_out[i], ln1_g[i], ln1_b[i], w_ffn_in[i], w_ffn_out[i],
                          ln2_g[i], ln2_b[i]) for i in range(DEPTH)]
    return (_trunk(x_prompt, layers), _trunk(x_sample, layers))
```

```python
import functools

import numpy as np
import jax
import jax.numpy as jnp
from jax import lax
from jax.experimental import pallas as pl
from jax.experimental.pallas import tpu as pltpu

D_MODEL = 1024
DEPTH = 2
GLA_HEADS = 4
GLA_DK = 64
GLA_DV = 128
GLA_QK_W = GLA_HEADS * GLA_DK
GLA_V_W = GLA_HEADS * GLA_DV
GATE_RANK = 16
GATE_TAU = 16.0
CHUNK = 64
NAT_HEADS = 8
NAT_HD = 64
NAT_W = NAT_HEADS * NAT_HD
GRID_W = 64
MAX_KR = 8
KC = 16
FFN_HIDDEN = 2816
ALPHA = (2 * DEPTH) ** 0.25
LN_EPS = 1e-5
MASK_VALUE = -1e30

LANES = 128
LR_PAD = LANES
FB_W = GLA_HEADS * 2 * GLA_DK
QUAD = 4
QUAD_W = QUAD * NAT_HD
N_QUADS = NAT_HEADS // QUAD
WIN_KEYS = MAX_KR * GRID_W
GLA_BLOCK_CHUNKS = 32
GLA_LOOKAHEAD = 2
NAT_ROWS_PER_ITER = 16
NAT_LOOKAHEAD = 2
PROJ_ROW_TILE = 1024
FFN_ROW_TILE = 512
FFN_ROW_GROUPS = (192, 192, 128)
LOG2E = 1.4426950408889634

VMEM_LIMIT = 56 * 1024 * 1024

QKL_W = 2 * GLA_QK_W + LR_PAD
PROJ_OUT_W = (QKL_W, GLA_V_W, GLA_V_W, NAT_W, NAT_W, NAT_W)
PROJ_OUT_DT = (jnp.float32, jnp.bfloat16, jnp.float32, jnp.bfloat16, jnp.bfloat16, jnp.bfloat16)
G_SPLIT = 2


def _cparams(n_grid):
    return pltpu.CompilerParams(
        dimension_semantics=("arbitrary",) * n_grid, vmem_limit_bytes=VMEM_LIMIT)


def _inproj_kernel(x_ref, w_ref, *o_refs):
    xb = x_ref[...].astype(jnp.bfloat16)
    off = 0
    for o_ref, width in zip(o_refs, PROJ_OUT_W):
        acc = jnp.dot(xb, w_ref[:, off:off + width], preferred_element_type=jnp.float32)
        o_ref[...] = acc.astype(o_ref.dtype)
        off += width


def _inproj(x2d, w_perm, tm):
    t = x2d.shape[0]
    wtot = w_perm.shape[1]
    return pl.pallas_call(
        _inproj_kernel,
        out_shape=[jax.ShapeDtypeStruct((t, w), dt) for w, dt in zip(PROJ_OUT_W, PROJ_OUT_DT)],
        grid=(t // tm,),
        in_specs=[pl.BlockSpec((tm, D_MODEL), lambda i: (i, 0)),
                  pl.BlockSpec((D_MODEL, wtot), lambda i: (0, 0))],
        out_specs=[pl.BlockSpec((tm, w), lambda i: (i, 0)) for w in PROJ_OUT_W],
        compiler_params=_cparams(1),
        name="inproj",
    )(x2d, w_perm)


def _fb_expand(x):
    lane = lax.broadcasted_iota(jnp.int32, (x.shape[0], LANES), 1)
    low = lane < GLA_DK
    slabs = []
    for p in range(GLA_HEADS // 2):
        pair = x[:, p * LANES:(p + 1) * LANES]
        swapped = pltpu.roll(pair, GLA_DK, axis=1)
        slabs.append(jnp.where(low, pair, swapped))
        slabs.append(jnp.where(low, swapped, pair))
    return jnp.concatenate(slabs, axis=1)


def _gla_kernel(qkl_ref, v_ref, r_ref, w2_ref, b2_ref, ng_ref, o_ref,
                qe_s, ke_s, dec_s, dst_s, stf_s, st_s, *, n_chunks):
    f32, bf16 = jnp.float32, jnp.bfloat16
    nb = GLA_BLOCK_CHUNKS
    n_blocks = n_chunks // nb
    blk = nb * CHUNK
    lane_fb = lax.broadcasted_iota(jnp.int32, (1, FB_W), 1)
    fwd_lane = (lane_fb % LANES) < GLA_DK
    fwd_lane_h = lax.broadcasted_iota(jnp.int32, (1, LANES), 1) < GLA_DK
    row = lax.broadcasted_iota(jnp.int32, (CHUNK, CHUNK), 0)
    col = lax.broadcasted_iota(jnp.int32, (CHUNK, CHUNK), 1)
    tril = row >= col
    tri_g = jnp.concatenate([tril.astype(bf16)] * G_SPLIT, axis=1)
    heads = [slice(h * LANES, (h + 1) * LANES) for h in range(GLA_HEADS)]
    chunks = [slice(j * CHUNK, (j + 1) * CHUNK) for j in range(nb)]

    st_s[...] = jnp.zeros_like(st_s)

    def pass1(b, carry):
        rows = pl.ds(pl.multiple_of(b * blk, blk), blk)
        c0 = b * nb
        z = jnp.dot(qkl_ref[rows, 2 * GLA_QK_W:].astype(bf16), w2_ref[...],
                    preferred_element_type=f32) + b2_ref[...]
        g = (jnp.minimum(z, 0.0) - jnp.log(1.0 + jnp.exp2(jnp.abs(z) * -LOG2E))) * (LOG2E / GATE_TAU)
        pieces, res = [], g
        for i in range(G_SPLIT):
            pieces.append(res.astype(bf16))
            if i + 1 < G_SPLIT:
                res = res - pieces[i].astype(f32)
        pre = [jnp.dot(tri_g, jnp.concatenate([p[cs] for p in pieces], axis=0),
                       preferred_element_type=f32) for cs in chunks]
        ke, dec = [], []
        for j, cs in enumerate(chunks):
            tot = pre[j][CHUNK - 1:CHUNK, :]
            dec.append(jnp.exp2(tot))
            dec_s[c0 + j] = dec[j]
            c_rows = pl.ds(pl.multiple_of((c0 + j) * CHUNK, CHUNK), CHUNK)
            edge = jnp.where(fwd_lane, tot - pre[j], pre[j] - g[cs])
            qq = _fb_expand(qkl_ref[c_rows, :GLA_QK_W])
            kk = _fb_expand(qkl_ref[c_rows, GLA_QK_W:2 * GLA_QK_W])
            qe_s[c_rows, :] = (qq * jnp.exp2(-edge)).astype(bf16)
            ke.append((kk * jnp.exp2(edge)).astype(bf16))
            ke_s[c_rows, :] = ke[j]
        d_st = [[lax.dot_general(v_ref[pl.ds(pl.multiple_of((c0 + j) * CHUNK, CHUNK), CHUNK), sl],
                                 ke[j][:, sl], (((0,), (0,)), ((), ())),
                                 preferred_element_type=f32)
                 for sl in heads] for j in range(nb)]
        for h, sl in enumerate(heads):
            st = st_s[h]
            for j in range(nb):
                dst_s[(c0 + j) * GLA_HEADS + h] = d_st[j][h]
                decayed = st * dec[j][:, sl]
                stf_s[(c0 + j) * GLA_HEADS + h] = decayed.astype(bf16)
                st = decayed + d_st[j][h]
            st_s[h] = st
        return carry

    lax.fori_loop(0, n_blocks, pass1, 0)

    st_s[...] = jnp.zeros_like(st_s)

    def pass2(i, carry):
        b = n_blocks - 1 - i
        rows = pl.ds(pl.multiple_of(b * blk, blk), blk)
        c0 = b * nb
        qe = qe_s[rows, :]
        ke = ke_s[rows, :]
        v_blk = v_ref[rows, :]
        st_c = [[None] * GLA_HEADS for _ in range(nb)]
        for h, sl in enumerate(heads):
            st = st_s[h]
            for j in reversed(range(nb)):
                idx = (c0 + j) * GLA_HEADS + h
                decayed = st * dec_s[c0 + j][:, sl]
                st_c[j][h] = jnp.where(fwd_lane_h, stf_s[idx], decayed.astype(bf16))
                st = decayed + dst_s[idx]
            st_s[h] = st
        zero = jnp.zeros((CHUNK, LANES), bf16)

        def scores(j):
            out = []
            for sl in heads:
                qe_h = qe[chunks[j], sl]
                lhs = jnp.concatenate([jnp.where(fwd_lane_h, qe_h, zero),
                                       jnp.where(fwd_lane_h, zero, qe_h)], axis=0)
                out.append(lax.dot_general(lhs, ke[chunks[j], sl], (((1,), (1,)), ((), ())),
                                           preferred_element_type=f32))
            return out

        r_blk = r_ref[rows, :]
        gate = r_blk * (1.0 / (1.0 + jnp.exp(-r_blk))) * ng_ref[...]
        order = list(reversed(range(nb)))
        pending = {j: scores(j) for j in order[:GLA_LOOKAHEAD]}
        for pos, j in enumerate(order):
            cs = chunks[j]
            sc = pending.pop(j)
            if pos + GLA_LOOKAHEAD < nb:
                pending[order[pos + GLA_LOOKAHEAD]] = scores(order[pos + GLA_LOOKAHEAD])
            c_rows = pl.ds(pl.multiple_of((c0 + j) * CHUNK, CHUNK), CHUNK)
            for h, sl in enumerate(heads):
                a = jnp.where(tril, sc[h][:CHUNK], sc[h][CHUNK:]).astype(bf16)
                o_h = jnp.dot(a, v_blk[cs, sl], preferred_element_type=f32)
                o_h = o_h + lax.dot_general(qe[cs, sl], st_c[j][h], (((1,), (1,)), ((), ())),
                                            preferred_element_type=f32)
                o_h = o_h * lax.rsqrt(jnp.mean(o_h * o_h, axis=-1, keepdims=True) + LN_EPS)
                o_ref[c_rows, sl] = (o_h * gate[cs, sl]).astype(o_ref.dtype)
        return carry

    lax.fori_loop(0, n_blocks, pass2, 0)


def _gla(qkl, v, r, w2p, b2p, ng):
    b, l, _ = qkl.shape
    n_chunks = l // CHUNK
    seq = lambda w: pl.BlockSpec((None, l, w), lambda i: (i, 0, 0))
    const = lambda s: pl.BlockSpec(s, lambda i: (0,) * len(s))
    return pl.pallas_call(
        functools.partial(_gla_kernel, n_chunks=n_chunks),
        out_shape=jax.ShapeDtypeStruct((b, l, GLA_V_W), jnp.bfloat16),
        grid=(b,),
        in_specs=[seq(QKL_W), seq(GLA_V_W), seq(GLA_V_W),
                  const((LR_PAD, FB_W)), const((1, FB_W)), const((1, GLA_V_W))],
        out_specs=seq(GLA_V_W),
        scratch_shapes=[pltpu.VMEM((l, FB_W), jnp.bfloat16),
                        pltpu.VMEM((l, FB_W), jnp.bfloat16),
                        pltpu.VMEM((n_chunks, 1, FB_W), jnp.float32),
                        pltpu.VMEM((n_chunks * GLA_HEADS, GLA_DV, LANES), jnp.float32),
                        pltpu.VMEM((n_chunks * GLA_HEADS, GLA_DV, LANES), jnp.bfloat16),
                        pltpu.VMEM((GLA_HEADS, GLA_DV, LANES), jnp.float32)],
        compiler_params=_cparams(1),
        name="gla",
    )(qkl, v, r, w2p, b2p, ng)


def _nat_kernel(q_ref, k_ref, v_ref, bias_ref, o_ref, *, n_rows):
    f32, bf16 = jnp.float32, jnp.bfloat16
    blk_r = lax.broadcasted_iota(jnp.int32, (QUAD_W, QUAD_W), 0) // NAT_HD
    blk_c = lax.broadcasted_iota(jnp.int32, (QUAD_W, QUAD_W), 1) // NAT_HD
    blockdiag = blk_r == blk_c
    out_head = lax.broadcasted_iota(jnp.int32, (GRID_W, QUAD_W), 1) // NAT_HD

    def rows_body(i, carry):
        items = []
        for t in range(NAT_ROWS_PER_ITER):
            r = i * NAT_ROWS_PER_ITER + t
            rs = jnp.clip(r - MAX_KR // 2, 0, n_rows - MAX_KR)
            q_rows = pl.ds(pl.multiple_of(r * GRID_W, GRID_W), GRID_W)
            w_rows = pl.ds(pl.multiple_of(rs * GRID_W, GRID_W), WIN_KEYS)
            for quad in range(N_QUADS):
                items.append((q_rows, w_rows, r - rs, quad,
                              slice(quad * QUAD_W, (quad + 1) * QUAD_W)))
        def score(item):
            q_rows, w_rows, e, quad, sl = item
            q_r = q_ref[q_rows, sl]
            q_bd = jnp.where(blockdiag, jnp.concatenate([q_r] * QUAD, axis=0),
                             jnp.zeros((QUAD_W, QUAD_W), bf16))
            return lax.dot_general(q_bd, k_ref[w_rows, sl], (((1,), (1,)), ((), ())),
                                   preferred_element_type=f32)

        def attend(item, s):
            q_rows, w_rows, e, quad, sl = item
            s = s + bias_ref[e, quad]
            p = jnp.exp2(s - jnp.max(s, axis=-1, keepdims=True))
            denom = jnp.sum(p, axis=-1, keepdims=True)
            return p.astype(bf16), denom

        def finish(item, p, denom):
            q_rows, w_rows, e, quad, sl = item
            o = jnp.dot(p, v_ref[w_rows, sl], preferred_element_type=f32)
            o = o * (1.0 / denom)
            acc = jnp.zeros((GRID_W, QUAD_W), f32)
            for h in range(QUAD):
                acc = acc + jnp.where(out_head == h, o[h * GRID_W:(h + 1) * GRID_W, :], 0.0)
            o_ref[q_rows, sl] = acc.astype(o_ref.dtype)

        pending = {k: score(items[k]) for k in range(min(NAT_LOOKAHEAD, len(items)))}
        for k, item in enumerate(items):
            p, denom = attend(item, pending.pop(k))
            if k + NAT_LOOKAHEAD < len(items):
                pending[k + NAT_LOOKAHEAD] = score(items[k + NAT_LOOKAHEAD])
            finish(item, p, denom)
        return carry

    lax.fori_loop(0, n_rows // NAT_ROWS_PER_ITER, rows_body, 0)


def _nat(q, k, v, bias):
    b, l, _ = q.shape
    n_rows = l // GRID_W
    seq = pl.BlockSpec((None, l, NAT_W), lambda i: (i, 0, 0))
    return pl.pallas_call(
        functools.partial(_nat_kernel, n_rows=n_rows),
        out_shape=jax.ShapeDtypeStruct((b, l, NAT_W), jnp.bfloat16),
        grid=(b,),
        in_specs=[seq, seq, seq,
                  pl.BlockSpec(bias.shape, lambda i: (0, 0, 0, 0))],
        out_specs=seq,
        compiler_params=_cparams(1),
        name="nat",
    )(q, k, v, bias)


def _nat_bias_table(rpb):
    c = np.arange(GRID_W)[:, None]
    m = np.arange(GRID_W)[None, :]
    col_start = np.clip(c - KC // 2, 0, GRID_W - KC)
    valid = (m >= col_start) & (m < col_start + KC)
    co = np.clip(m - c + KC - 1, 0, 2 * KC - 2)
    onehot = (co[:, :, None] == np.arange(2 * KC - 1)).astype(np.float32)
    toep = jnp.einsum("hrt,cmt->hcrm", rpb * LOG2E, jnp.asarray(onehot),
                      precision=lax.Precision.HIGHEST)
    toep = jnp.where(jnp.asarray(valid)[None, :, None, :], toep, MASK_VALUE)
    t = jnp.stack([toep[:, :, MAX_KR - 1 - e:2 * MAX_KR - 1 - e] for e in range(MAX_KR)])
    return t.reshape(MAX_KR, N_QUADS, QUAD * GRID_W, MAX_KR * GRID_W).astype(jnp.float32)


def _layer_norm(y, g, b):
    mu = jnp.mean(y, axis=-1, keepdims=True)
    yc = y - mu
    var = jnp.mean(yc * yc, axis=-1, keepdims=True)
    return yc * lax.rsqrt(var + LN_EPS) * g + b


def _mix_ffn_kernel(oa_ref, on_ref, x_ref, wa_ref, wn_ref, g1_ref, b1_ref,
                    w1_ref, w2_ref, g2_ref, b2_ref, o_ref):
    f32, bf16 = jnp.float32, jnp.bfloat16
    assert sum(FFN_ROW_GROUPS) == x_ref.shape[0]
    starts = np.cumsum((0,) + FFN_ROW_GROUPS)
    groups = [slice(int(a), int(b)) for a, b in zip(starts[:-1], starts[1:])]
    mix = [jnp.dot(oa_ref[rows, :], wa_ref[...], preferred_element_type=f32)
           + jnp.dot(on_ref[rows, :], wn_ref[...], preferred_element_type=f32) for rows in groups]
    x1 = [_layer_norm(ALPHA * x_ref[rows, :] + m, g1_ref[...], b1_ref[...])
          for rows, m in zip(groups, mix)]
    h = [jnp.dot(x.astype(bf16), w1_ref[...], preferred_element_type=f32) for x in x1]
    act = []
    for hh in h:
        gate = hh[:, :FFN_HIDDEN]
        act.append((gate * (1.0 / (1.0 + jnp.exp(-gate))) * hh[:, FFN_HIDDEN:]).astype(bf16))
    y = [jnp.dot(a, w2_ref[...], preferred_element_type=f32) for a in act]
    for rows, x, yy in zip(groups, x1, y):
        o_ref[rows, :] = _layer_norm(ALPHA * x + yy, g2_ref[...], b2_ref[...])


def _mix_ffn(oa, on, x2d, p, tm):
    t = x2d.shape[0]
    tile = lambda w: pl.BlockSpec((tm, w), lambda i: (i, 0))
    const = lambda s: pl.BlockSpec(s, lambda i: (0, 0), pipeline_mode=pl.Buffered(1))
    vec = const((1, D_MODEL))
    return pl.pallas_call(
        _mix_ffn_kernel,
        out_shape=jax.ShapeDtypeStruct((t, D_MODEL), jnp.float32),
        grid=(t // tm,),
        in_specs=[tile(GLA_V_W), tile(NAT_W), tile(D_MODEL),
                  const((GLA_V_W, D_MODEL)), const((NAT_W, D_MODEL)), vec, vec,
                  const((D_MODEL, 2 * FFN_HIDDEN)), const((FFN_HIDDEN, D_MODEL)), vec, vec],
        out_specs=tile(D_MODEL),
        compiler_params=_cparams(1),
        name="mix_ffn",
    )(oa, on, x2d, p["wa"], p["wn"], p["ln1_g"], p["ln1_b"],
      p["w1"], p["w2"], p["ln2_g"], p["ln2_b"])


def _prep_layer(w_in, gate_w2, gate_b, norm_g, rpb, w_out, ln1_g, ln1_b,
                w_ffn_in, w_ffn_out, ln2_g, ln2_b):
    f32, bf16 = jnp.float32, jnp.bfloat16
    sizes = (GLA_QK_W, GLA_QK_W, GLA_V_W, GLA_V_W, 2 * GATE_RANK, NAT_W, NAT_W, NAT_W)
    pts = np.cumsum((0,) + sizes)
    q_a, k_a, v_a, r_a, lr, q_b, k_b, v_b = [w_in[:, pts[i]:pts[i + 1]] for i in range(8)]
    lr = jnp.pad(lr, ((0, 0), (0, LR_PAD - 2 * GATE_RANK)))
    w_perm = jnp.concatenate(
        [q_a * GLA_DK ** -0.5, k_a, lr, v_a, r_a, q_b * (NAT_HD ** -0.5 * LOG2E), k_b, v_b],
        axis=1)
    w2p = jnp.zeros((LR_PAD, GLA_HEADS, 2, GLA_DK), f32)
    w2p = w2p.at[:GATE_RANK, :, 0, :].set(gate_w2[0].reshape(GATE_RANK, GLA_HEADS, GLA_DK))
    w2p = w2p.at[GATE_RANK:2 * GATE_RANK, :, 1, :].set(
        gate_w2[1].reshape(GATE_RANK, GLA_HEADS, GLA_DK))
    b2p = jnp.stack([gate_b[0].reshape(GLA_HEADS, GLA_DK),
                     gate_b[1].reshape(GLA_HEADS, GLA_DK)], axis=1)
    return dict(
        w_perm=w_perm.astype(bf16),
        w2p=w2p.reshape(LR_PAD, FB_W).astype(bf16),
        b2p=b2p.reshape(1, FB_W).astype(f32),
        ng=norm_g.reshape(1, GLA_V_W).astype(f32),
        bias=_nat_bias_table(rpb.astype(f32)),
        wa=w_out[:GLA_V_W].astype(bf16), wn=w_out[GLA_V_W:].astype(bf16),
        ln1_g=ln1_g.reshape(1, D_MODEL), ln1_b=ln1_b.reshape(1, D_MODEL),
        w1=w_ffn_in.astype(bf16), w2=w_ffn_out.astype(bf16),
        ln2_g=ln2_g.reshape(1, D_MODEL), ln2_b=ln2_b.reshape(1, D_MODEL),
    )


def _row_tile(t, want):
    return want if t % want == 0 else t


def _layer(x2d, b, l, p):
    tm = _row_tile(x2d.shape[0], FFN_ROW_TILE)
    qkl, v_a, r_a, q_b, k_b, v_b = _inproj(x2d, p["w_perm"], _row_tile(x2d.shape[0], PROJ_ROW_TILE))
    seq = lambda a: a.reshape(b, l, a.shape[-1])
    o_a = _gla(seq(qkl), seq(v_a), seq(r_a), p["w2p"], p["b2p"], p["ng"])
    o_n = _nat(seq(q_b), seq(k_b), seq(v_b), p["bias"])
    return _mix_ffn(o_a.reshape(b * l, GLA_V_W), o_n.reshape(b * l, NAT_W), x2d, p, tm)


def _trunk(x, layers):
    b, l, d = x.shape
    x2d = x.reshape(b * l, d)
    for p in layers:
        x2d = _layer(x2d, b, l, p)
    return x2d.reshape(b, l, d)


def kernel(x_prompt, x_sample, w_in, gla_gate_w2, gla_gate_b, gla_norm_g, nat_rpb, w_out,
           ln1_g, ln1_b, w_ffn_in, w_ffn_out, ln2_g, ln2_b):
    layers = [_prep_layer(w_in[i], gla_gate_w2[i], gla_gate_b[i], gla_norm_g[i], nat_rpb[i],
                          w_out[i], ln1_g[i], ln1_b[i], w_ffn_in[i], w_ffn_out[i],
                          ln2_g[i], ln2_b[i]) for i in range(DEPTH)]
    return (_trunk(x_prompt, layers), _trunk(x_sample, layers))
```

```python
import functools

import numpy as np
import jax
import jax.numpy as jnp
from jax import lax
from jax.experimental import pallas as pl
from jax.experimental.pallas import tpu as pltpu

D_MODEL = 1024
DEPTH = 2
GLA_HEADS = 4
GLA_DK = 64
GLA_DV = 128
GLA_QK_W = GLA_HEADS * GLA_DK
GLA_V_W = GLA_HEADS * GLA_DV
GATE_RANK = 16
GATE_TAU = 16.0
CHUNK = 64
NAT_HEADS = 8
NAT_HD = 64
NAT_W = NAT_HEADS * NAT_HD
GRID_W = 64
MAX_KR = 8
KC = 16
FFN_HIDDEN = 2816
ALPHA = (2 * DEPTH) ** 0.25
LN_EPS = 1e-5
MASK_VALUE = -1e30

LANES = 128
LR_PAD = LANES
FB_W = GLA_HEADS * 2 * GLA_DK
QUAD = 4
QUAD_W = QUAD * NAT_HD
N_QUADS = NAT_HEADS // QUAD
WIN_KEYS = MAX_KR * GRID_W
GLA_BLOCK_CHUNKS = 32
GLA_LOOKAHEAD = 2
NAT_ROWS_PER_ITER = 16
NAT_LOOKAHEAD = 2
PROJ_ROW_TILE = 1024
FFN_ROW_TILE = 512
FFN_ROW_GROUPS = (192, 192, 128)
LOG2E = 1.4426950408889634

VMEM_LIMIT = 56 * 1024 * 1024

QKL_W = 2 * GLA_QK_W + LR_PAD
PROJ_OUT_W = (QKL_W, GLA_V_W, GLA_V_W, NAT_W, NAT_W, NAT_W)
PROJ_OUT_DT = (jnp.float32, jnp.bfloat16, jnp.float32, jnp.bfloat16, jnp.bfloat16, jnp.bfloat16)
G_SPLIT = 2


def _cparams(n_grid):
    return pltpu.CompilerParams(
        dimension_semantics=("arbitrary",) * n_grid, vmem_limit_bytes=VMEM_LIMIT)


def _inproj_kernel(x_ref, w_ref, *o_refs):
    xb = x_ref[...].astype(jnp.bfloat16)
    off = 0
    for o_ref, width in zip(o_refs, PROJ_OUT_W):
        acc = jnp.dot(xb, w_ref[:, off:off + width], preferred_element_type=jnp.float32)
        o_ref[...] = acc.astype(o_ref.dtype)
        off += width


def _inproj(x2d, w_perm, tm):
    t = x2d.shape[0]
    wtot = w_perm.shape[1]
    return pl.pallas_call(
        _inproj_kernel,
        out_shape=[jax.ShapeDtypeStruct((t, w), dt) for w, dt in zip(PROJ_OUT_W, PROJ_OUT_DT)],
        grid=(t // tm,),
        in_specs=[pl.BlockSpec((tm, D_MODEL), lambda i: (i, 0)),
                  pl.BlockSpec((D_MODEL, wtot), lambda i: (0, 0))],
        out_specs=[pl.BlockSpec((tm, w), lambda i: (i, 0)) for w in PROJ_OUT_W],
        compiler_params=_cparams(1),
        name="inproj",
    )(x2d, w_perm)


def _fb_expand(x):
    lane = lax.broadcasted_iota(jnp.int32, (x.shape[0], LANES), 1)
    low = lane < GLA_DK
    slabs = []
    for p in range(GLA_HEADS // 2):
        pair = x[:, p * LANES:(p + 1) * LANES]
        swapped = pltpu.roll(pair, GLA_DK, axis=1)
        slabs.append(jnp.where(low, pair, swapped))
        slabs.append(jnp.where(low, swapped, pair))
    return jnp.concatenate(slabs, axis=1)


def _gla_kernel(qkl_ref, v_ref, r_ref, w2_ref, b2_ref, ng_ref, o_ref,
                qe_s, ke_s, dec_s, dst_s, stf_s, st_s, *, n_chunks):
    f32, bf16 = jnp.float32, jnp.bfloat16
    nb = GLA_BLOCK_CHUNKS
    n_blocks = n_chunks // nb
    blk = nb * CHUNK
    lane_fb = lax.broadcasted_iota(jnp.int32, (1, FB_W), 1)
    fwd_lane = (lane_fb % LANES) < GLA_DK
    fwd_lane_h = lax.broadcasted_iota(jnp.int32, (1, LANES), 1) < GLA_DK
    row = lax.broadcasted_iota(jnp.int32, (CHUNK, CHUNK), 0)
    col = lax.broadcasted_iota(jnp.int32, (CHUNK, CHUNK), 1)
    tril = row >= col
    tri_g = jnp.concatenate([tril.astype(bf16)] * G_SPLIT, axis=1)
    heads = [slice(h * LANES, (h + 1) * LANES) for h in range(GLA_HEADS)]
    chunks = [slice(j * CHUNK, (j + 1) * CHUNK) for j in range(nb)]

    st_s[...] = jnp.zeros_like(st_s)

    def pass1(b, carry):
        rows = pl.ds(pl.multiple_of(b * blk, blk), blk)
        c0 = b * nb
        z = jnp.dot(qkl_ref[rows, 2 * GLA_QK_W:].astype(bf16), w2_ref[...],
                    preferred_element_type=f32) + b2_ref[...]
        g = (jnp.minimum(z, 0.0) - jnp.log(1.0 + jnp.exp2(jnp.abs(z) * -LOG2E))) * (LOG2E / GATE_TAU)
        pieces, res = [], g
        for i in range(G_SPLIT):
            pieces.append(res.astype(bf16))
            if i + 1 < G_SPLIT:
                res = res - pieces[i].astype(f32)
        pre = [jnp.dot(tri_g, jnp.concatenate([p[cs] for p in pieces], axis=0),
                       preferred_element_type=f32) for cs in chunks]
        ke, dec = [], []
        for j, cs in enumerate(chunks):
            tot = pre[j][CHUNK - 1:CHUNK, :]
            dec.append(jnp.exp2(tot))
            dec_s[c0 + j] = dec[j]
            c_rows = pl.ds(pl.multiple_of((c0 + j) * CHUNK, CHUNK), CHUNK)
            edge = jnp.where(fwd_lane, tot - pre[j], pre[j] - g[cs])
            qq = _fb_expand(qkl_ref[c_rows, :GLA_QK_W])
            kk = _fb_expand(qkl_ref[c_rows, GLA_QK_W:2 * GLA_QK_W])
            qe_s[c_rows, :] = (qq * jnp.exp2(-edge)).astype(bf16)
            ke.append((kk * jnp.exp2(edge)).astype(bf16))
            ke_s[c_rows, :] = ke[j]
        d_st = [[lax.dot_general(v_ref[pl.ds(pl.multiple_of((c0 + j) * CHUNK, CHUNK), CHUNK), sl],
                                 ke[j][:, sl], (((0,), (0,)), ((), ())),
                                 preferred_element_type=f32)
                 for sl in heads] for j in range(nb)]
        for h, sl in enumerate(heads):
            st = st_s[h]
            for j in range(nb):
                dst_s[(c0 + j) * GLA_HEADS + h] = d_st[j][h]
                decayed = st * dec[j][:, sl]
                stf_s[(c0 + j) * GLA_HEADS + h] = decayed.astype(bf16)
                st = decayed + d_st[j][h]
            st_s[h] = st
        return carry

    lax.fori_loop(0, n_blocks, pass1, 0)

    st_s[...] = jnp.zeros_like(st_s)

    def pass2(i, carry):
        b = n_blocks - 1 - i
        rows = pl.ds(pl.multiple_of(b * blk, blk), blk)
        c0 = b * nb
        qe = qe_s[rows, :]
        ke = ke_s[rows, :]
        v_blk = v_ref[rows, :]
        st_c = [[None] * GLA_HEADS for _ in range(nb)]
        for h, sl in enumerate(heads):
            st = st_s[h]
            for j in reversed(range(nb)):
                idx = (c0 + j) * GLA_HEADS + h
                decayed = st * dec_s[c0 + j][:, sl]
                st_c[j][h] = jnp.where(fwd_lane_h, stf_s[idx], decayed.astype(bf16))
                st = decayed + dst_s[idx]
            st_s[h] = st
        zero = jnp.zeros((CHUNK, LANES), bf16)

        def scores(j):
            out = []
            for sl in heads:
                qe_h = qe[chunks[j], sl]
                lhs = jnp.concatenate([jnp.where(fwd_lane_h, qe_h, zero),
                                       jnp.where(fwd_lane_h, zero, qe_h)], axis=0)
                out.append(lax.dot_general(lhs, ke[chunks[j], sl], (((1,), (1,)), ((), ())),
                                           preferred_element_type=f32))
            return out

        r_blk = r_ref[rows, :]
        gate = r_blk * (1.0 / (1.0 + jnp.exp(-r_blk))) * ng_ref[...]
        order = list(reversed(range(nb)))
        pending = {j: scores(j) for j in order[:GLA_LOOKAHEAD]}
        for pos, j in enumerate(order):
            cs = chunks[j]
            sc = pending.pop(j)
            if pos + GLA_LOOKAHEAD < nb:
                pending[order[pos + GLA_LOOKAHEAD]] = scores(order[pos + GLA_LOOKAHEAD])
            c_rows = pl.ds(pl.multiple_of((c0 + j) * CHUNK, CHUNK), CHUNK)
            for h, sl in enumerate(heads):
                a = jnp.where(tril, sc[h][:CHUNK], sc[h][CHUNK:]).astype(bf16)
                o_h = jnp.dot(a, v_blk[cs, sl], preferred_element_type=f32)
                o_h = o_h + lax.dot_general(qe[cs, sl], st_c[j][h], (((1,), (1,)), ((), ())),
                                            preferred_element_type=f32)
                o_h = o_h * lax.rsqrt(jnp.mean(o_h * o_h, axis=-1, keepdims=True) + LN_EPS)
                o_ref[c_rows, sl] = (o_h * gate[cs, sl]).astype(o_ref.dtype)
        return carry

    lax.fori_loop(0, n_blocks, pass2, 0)


def _gla(qkl, v, r, w2p, b2p, ng):
    b, l, _ = qkl.shape
    n_chunks = l // CHUNK
    seq = lambda w: pl.BlockSpec((None, l, w), lambda i: (i, 0, 0))
    const = lambda s: pl.BlockSpec(s, lambda i: (0,) * len(s))
    return pl.pallas_call(
        functools.partial(_gla_kernel, n_chunks=n_chunks),
        out_shape=jax.ShapeDtypeStruct((b, l, GLA_V_W), jnp.bfloat16),
        grid=(b,),
        in_specs=[seq(QKL_W), seq(GLA_V_W), seq(GLA_V_W),
                  const((LR_PAD, FB_W)), const((1, FB_W)), const((1, GLA_V_W))],
        out_specs=seq(GLA_V_W),
        scratch_shapes=[pltpu.VMEM((l, FB_W), jnp.bfloat16),
                        pltpu.VMEM((l, FB_W), jnp.bfloat16),
                        pltpu.VMEM((n_chunks, 1, FB_W), jnp.float32),
                        pltpu.VMEM((n_chunks * GLA_HEADS, GLA_DV, LANES), jnp.float32),
                        pltpu.VMEM((n_chunks * GLA_HEADS, GLA_DV, LANES), jnp.bfloat16),
                        pltpu.VMEM((GLA_HEADS, GLA_DV, LANES), jnp.float32)],
        compiler_params=_cparams(1),
        name="gla",
    )(qkl, v, r, w2p, b2p, ng)


def _nat_kernel(q_ref, k_ref, v_ref, bias_ref, o_ref, *, n_rows):
    f32, bf16 = jnp.float32, jnp.bfloat16
    blk_r = lax.broadcasted_iota(jnp.int32, (QUAD_W, QUAD_W), 0) // NAT_HD
    blk_c = lax.broadcasted_iota(jnp.int32, (QUAD_W, QUAD_W), 1) // NAT_HD
    blockdiag = blk_r == blk_c
    out_head = lax.broadcasted_iota(jnp.int32, (GRID_W, QUAD_W), 1) // NAT_HD

    def rows_body(i, carry):
        items = []
        for t in range(NAT_ROWS_PER_ITER):
            r = i * NAT_ROWS_PER_ITER + t
            rs = jnp.clip(r - MAX_KR // 2, 0, n_rows - MAX_KR)
            q_rows = pl.ds(pl.multiple_of(r * GRID_W, GRID_W), GRID_W)
            w_rows = pl.ds(pl.multiple_of(rs * GRID_W, GRID_W), WIN_KEYS)
            for quad in range(N_QUADS):
                items.append((q_rows, w_rows, r - rs, quad,
                              slice(quad * QUAD_W, (quad + 1) * QUAD_W)))
        def score(item):
            q_rows, w_rows, e, quad, sl = item
            q_r = q_ref[q_rows, sl]
            q_bd = jnp.where(blockdiag, jnp.concatenate([q_r] * QUAD, axis=0),
                             jnp.zeros((QUAD_W, QUAD_W), bf16))
            return lax.dot_general(q_bd, k_ref[w_rows, sl], (((1,), (1,)), ((), ())),
                                   preferred_element_type=f32)

        def attend(item, s):
            q_rows, w_rows, e, quad, sl = item
            s = s + bias_ref[e, quad]
            p = jnp.exp2(s - jnp.max(s, axis=-1, keepdims=True))
            denom = jnp.sum(p, axis=-1, keepdims=True)
            return p.astype(bf16), denom

        def finish(item, p, denom):
            q_rows, w_rows, e, quad, sl = item
            o = jnp.dot(p, v_ref[w_rows, sl], preferred_element_type=f32)
            o = o * (1.0 / denom)
            acc = jnp.zeros((GRID_W, QUAD_W), f32)
            for h in range(QUAD):
                acc = acc + jnp.where(out_head == h, o[h * GRID_W:(h + 1) * GRID_W, :], 0.0)
            o_ref[q_rows, sl] = acc.astype(o_ref.dtype)

        pending = {k: score(items[k]) for k in range(min(NAT_LOOKAHEAD, len(items)))}
        for k, item in enumerate(items):
            p, denom = attend(item, pending.pop(k))
            if k + NAT_LOOKAHEAD < len(items):
                pending[k + NAT_LOOKAHEAD] = score(items[k + NAT_LOOKAHEAD])
            finish(item, p, denom)
        return carry

    lax.fori_loop(0, n_rows // NAT_ROWS_PER_ITER, rows_body, 0)


def _nat(q, k, v, bias):
    b, l, _ = q.shape
    n_rows = l // GRID_W
    seq = pl.BlockSpec((None, l, NAT_W), lambda i: (i, 0, 0))
    return pl.pallas_call(
        functools.partial(_nat_kernel, n_rows=n_rows),
        out_shape=jax.ShapeDtypeStruct((b, l, NAT_W), jnp.bfloat16),
        grid=(b,),
        in_specs=[seq, seq, seq,
                  pl.BlockSpec(bias.shape, lambda i: (0, 0, 0, 0))],
        out_specs=seq,
        compiler_params=_cparams(1),
        name="nat",
    )(q, k, v, bias)


def _nat_bias_table(rpb):
    c = np.arange(GRID_W)[:, None]
    m = np.arange(GRID_W)[None, :]
    col_start = np.clip(c - KC // 2, 0, GRID_W - KC)
    valid = (m >= col_start) & (m < col_start + KC)
    co = np.clip(m - c + KC - 1, 0, 2 * KC - 2)
    onehot = (co[:, :, None] == np.arange(2 * KC - 1)).astype(np.float32)
    toep = jnp.einsum("hrt,cmt->hrcm", rpb * LOG2E, jnp.asarray(onehot),
                      precision=lax.Precision.HIGHEST)
    toep = jnp.where(jnp.asarray(valid)[None, None], toep, MASK_VALUE)
    t = jnp.stack([toep[:, MAX_KR - 1 - e:2 * MAX_KR - 1 - e] for e in range(MAX_KR)])
    t = jnp.transpose(t, (0, 1, 3, 2, 4))
    return t.reshape(MAX_KR, N_QUADS, QUAD * GRID_W, MAX_KR * GRID_W).astype(jnp.float32)


def _layer_norm(y, g, b):
    mu = jnp.mean(y, axis=-1, keepdims=True)
    yc = y - mu
    var = jnp.mean(yc * yc, axis=-1, keepdims=True)
    return yc * lax.rsqrt(var + LN_EPS) * g + b


def _mix_ffn_kernel(oa_ref, on_ref, x_ref, wa_ref, wn_ref, g1_ref, b1_ref,
                    w1_ref, w2_ref, g2_ref, b2_ref, o_ref):
    f32, bf16 = jnp.float32, jnp.bfloat16
    assert sum(FFN_ROW_GROUPS) == x_ref.shape[0]
    starts = np.cumsum((0,) + FFN_ROW_GROUPS)
    groups = [slice(int(a), int(b)) for a, b in zip(starts[:-1], starts[1:])]
    mix = [jnp.dot(oa_ref[rows, :], wa_ref[...], preferred_element_type=f32)
           + jnp.dot(on_ref[rows, :], wn_ref[...], preferred_element_type=f32) for rows in groups]
    x1 = [_layer_norm(ALPHA * x_ref[rows, :] + m, g1_ref[...], b1_ref[...])
          for rows, m in zip(groups, mix)]
    h = [jnp.dot(x.astype(bf16), w1_ref[...], preferred_element_type=f32) for x in x1]
    act = []
    for hh in h:
        gate = hh[:, :FFN_HIDDEN]
        act.append((gate * (1.0 / (1.0 + jnp.exp(-gate))) * hh[:, FFN_HIDDEN:]).astype(bf16))
    y = [jnp.dot(a, w2_ref[...], preferred_element_type=f32) for a in act]
    for rows, x, yy in zip(groups, x1, y):
        o_ref[rows, :] = _layer_norm(ALPHA * x + yy, g2_ref[...], b2_ref[...])


def _mix_ffn(oa, on, x2d, p, tm):
    t = x2d.shape[0]
    tile = lambda w: pl.BlockSpec((tm, w), lambda i: (i, 0))
    const = lambda s: pl.BlockSpec(s, lambda i: (0, 0), pipeline_mode=pl.Buffered(1))
    vec = const((1, D_MODEL))
    return pl.pallas_call(
        _mix_ffn_kernel,
        out_shape=jax.ShapeDtypeStruct((t, D_MODEL), jnp.float32),
        grid=(t // tm,),
        in_specs=[tile(GLA_V_W), tile(NAT_W), tile(D_MODEL),
                  const((GLA_V_W, D_MODEL)), const((NAT_W, D_MODEL)), vec, vec,
                  const((D_MODEL, 2 * FFN_HIDDEN)), const((FFN_HIDDEN, D_MODEL)), vec, vec],
        out_specs=tile(D_MODEL),
        compiler_params=_cparams(1),
        name="mix_ffn",
    )(oa, on, x2d, p["wa"], p["wn"], p["ln1_g"], p["ln1_b"],
      p["w1"], p["w2"], p["ln2_g"], p["ln2_b"])


def _prep_layer(w_in, gate_w2, gate_b, norm_g, rpb, w_out, ln1_g, ln1_b,
                w_ffn_in, w_ffn_out, ln2_g, ln2_b):
    f32, bf16 = jnp.float32, jnp.bfloat16
    sizes = (GLA_QK_W, GLA_QK_W, GLA_V_W, GLA_V_W, 2 * GATE_RANK, NAT_W, NAT_W, NAT_W)
    pts = np.cumsum((0,) + sizes)
    q_a, k_a, v_a, r_a, lr, q_b, k_b, v_b = [w_in[:, pts[i]:pts[i + 1]] for i in range(8)]
    lr = jnp.pad(lr, ((0, 0), (0, LR_PAD - 2 * GATE_RANK)))
    w_perm = jnp.concatenate(
        [q_a * GLA_DK ** -0.5, k_a, lr, v_a, r_a, q_b * (NAT_HD ** -0.5 * LOG2E), k_b, v_b],
        axis=1)
    w2p = jnp.zeros((LR_PAD, GLA_HEADS, 2, GLA_DK), f32)
    w2p = w2p.at[:GATE_RANK, :, 0, :].set(gate_w2[0].reshape(GATE_RANK, GLA_HEADS, GLA_DK))
    w2p = w2p.at[GATE_RANK:2 * GATE_RANK, :, 1, :].set(
        gate_w2[1].reshape(GATE_RANK, GLA_HEADS, GLA_DK))
    b2p = jnp.stack([gate_b[0].reshape(GLA_HEADS, GLA_DK),
                     gate_b[1].reshape(GLA_HEADS, GLA_DK)], axis=1)
    return dict(
        w_perm=w_perm.astype(bf16),
        w2p=w2p.reshape(LR_PAD, FB_W).astype(bf16),
        b2p=b2p.reshape(1, FB_W).astype(f32),
        ng=norm_g.reshape(1, GLA_V_W).astype(f32),
        bias=_nat_bias_table(rpb.astype(f32)),
        wa=w_out[:GLA_V_W].astype(bf16), wn=w_out[GLA_V_W:].astype(bf16),
        ln1_g=ln1_g.reshape(1, D_MODEL), ln1_b=ln1_b.reshape(1, D_MODEL),
        w1=w_ffn_in.astype(bf16), w2=w_ffn_out.astype(bf16),
        ln2_g=ln2_g.reshape(1, D_MODEL), ln2_b=ln2_b.reshape(1, D_MODEL),
    )


def _row_tile(t, want):
    return want if t % want == 0 else t


def _layer(x2d, b, l, p):
    tm = _row_tile(x2d.shape[0], FFN_ROW_TILE)
    qkl, v_a, r_a, q_b, k_b, v_b = _inproj(x2d, p["w_perm"], _row_tile(x2d.shape[0], PROJ_ROW_TILE))
    seq = lambda a: a.reshape(b, l, a.shape[-1])
    o_a = _gla(seq(qkl), seq(v_a), seq(r_a), p["w2p"], p["b2p"], p["ng"])
    o_n = _nat(seq(q_b), seq(k_b), seq(v_b), p["bias"])
    return _mix_ffn(o_a.reshape(b * l, GLA_V_W), o_n.reshape(b * l, NAT_W), x2d, p, tm)


def _trunk(x, layers):
    b, l, d = x.shape
    x2d = x.reshape(b * l, d)
    for p in layers:
        x2d = _layer(x2d, b, l, p)
    return x2d.reshape(b, l, d)


def kernel(x_prompt, x_sample, w_in, gla_gate_w2, gla_gate_b, gla_norm_g, nat_rpb, w_out,
           ln1_g, ln1_b, w_ffn_in, w_ffn_out, ln2_g, ln2_b):
    layers = [_prep_layer(w_in[i], gla_gate_w2[i], gla_gate_b[i], gla_norm_g[i], nat_rpb[i],
                          w_out[i], ln1_g[i], ln1_b[i], w_ffn_in[i], w_ffn_out[i],
                          ln2_g[i], ln2_b[i]) for i in range(DEPTH)]
    return (_trunk(x_prompt, layers), _trunk(x_sample, layers))
```
